```python
import math
import jax, jax.numpy as jnp
from jax import lax
import numpy as np

D_MODEL = 1024
BATCH = 2
SEQ = 8192
DEPTH = 1

PLE_DIM = 256
MIX_WIDTH = D_MODEL
POOL_WIDTH = MIX_WIDTH // 2
POOL_WINDOWS = (2, 4, 8, 16)
N_POOL_GROUPS = len(POOL_WINDOWS)
POOL_GROUP = POOL_WIDTH // N_POOL_GROUPS
HEAD_DIM = 64
ATTN_WIDTH = MIX_WIDTH - POOL_WIDTH
N_Q_HEADS = ATTN_WIDTH // HEAD_DIM
N_KV_HEADS = 2
GQA_GROUP = N_Q_HEADS // N_KV_HEADS
KV_WIDTH = N_KV_HEADS * HEAD_DIM
IN_WIDTH = POOL_WIDTH + ATTN_WIDTH + 2 * KV_WIDTH
WINDOW = 128
BLOCK = 128
D_FF = 4 * D_MODEL
LN_EPS = 1e-5
DEEPNORM_ALPHA = (2 * DEPTH) ** 0.25
DEEPNORM_BETA = (8 * DEPTH) ** -0.25
NEG_INF = -1e30

kernel_name = "hymba_pool_swa_deepnorm_layer"


def layer_norm(x, g, b):
    xf = x.astype(jnp.float32)
    mu = jnp.mean(xf, axis=-1, keepdims=True)
    xc = xf - mu
    var = jnp.mean(xc * xc, axis=-1, keepdims=True)
    y = xc * lax.rsqrt(var + LN_EPS) * g.astype(jnp.float32) + b.astype(jnp.float32)
    return y.astype(x.dtype)


def alibi_slopes(n_heads):
    h = jnp.arange(1, n_heads + 1, dtype=jnp.float32)
    return jnp.exp2(-8.0 * h / n_heads)


def pool_mixer(u, w_pool, pool_scale):
    B, S, _ = u.shape
    uf = u.astype(jnp.float32)
    cs = jnp.pad(jnp.cumsum(uf, axis=1), ((0, 0), (1, 0), (0, 0)))
    t = jnp.arange(S)
    outs = []
    for g, w in enumerate(POOL_WINDOWS):
        sl = slice(g * POOL_GROUP, (g + 1) * POOL_GROUP)
        c = cs[..., sl]
        lo = jnp.maximum(t + 1 - w, 0)
        win_sum = c[:, 1:] - c[:, lo]
        count = (t + 1 - lo).astype(jnp.float32)[None, :, None]
        outs.append(win_sum / count - uf[..., sl])
    d = jnp.stack(outs, axis=2).astype(u.dtype)
    y = jnp.einsum('bsgc,gcd->bsgd', d, w_pool).reshape(B, S, POOL_WIDTH)
    return y * pool_scale


def sliding_window_attention(q, k, v, sinks):
    B, S = q.shape[:2]
    nb = S // BLOCK
    qb = q.reshape(B, nb, BLOCK, N_KV_HEADS, GQA_GROUP, HEAD_DIM)

    def band(a):
        ab = a.reshape(B, nb, BLOCK, N_KV_HEADS, HEAD_DIM)
        prev = jnp.pad(ab, ((0, 0), (1, 0), (0, 0), (0, 0), (0, 0)))[:, :-1]
        return jnp.concatenate([prev, ab], axis=2)

    kb, vb = band(k), band(v)
    scale = 1.0 / math.sqrt(HEAD_DIM)
    scores = jnp.einsum('bnqkgd,bnjkd->bnkgqj', qb, kb).astype(jnp.float32) * scale

    qi = jnp.arange(BLOCK)
    kj = jnp.arange(2 * BLOCK)
    dist = (qi[:, None] + BLOCK - kj[None, :])
    in_band = (dist >= 0) & (dist < WINDOW)
    key_pos = jnp.arange(nb)[:, None] * BLOCK - BLOCK + kj[None, :]
    mask = in_band[None] & (key_pos >= 0)[:, None, :]

    slopes = alibi_slopes(N_Q_HEADS).reshape(N_KV_HEADS, GQA_GROUP)
    bias = -slopes[:, :, None, None] * dist.astype(jnp.float32)[None, None]
    scores = jnp.where(mask[None, :, None, None], scores + bias[None, None], NEG_INF)

    sink = sinks.astype(jnp.float32).reshape(1, 1, N_KV_HEADS, GQA_GROUP, 1, 1)
    m = jnp.maximum(jnp.max(scores, axis=-1, keepdims=True), sink)
    e = jnp.exp(scores - m)
    denom = jnp.sum(e, axis=-1, keepdims=True) + jnp.exp(sink - m)
    probs = (e / denom).astype(v.dtype)
    out = jnp.einsum('bnkgqj,bnjkd->bnqkgd', probs, vb)
    return out.reshape(B, S, ATTN_WIDTH)


def setup_inputs(seed: int = 0) -> dict:
    key = jax.random.key(seed)
    ks = jax.random.split(key, 16)
    f32 = jnp.float32

    def nrm(k, shape, std):
        return jax.random.normal(k, shape, f32) * std

    return {
        "x": nrm(ks[0], (BATCH, SEQ, D_MODEL), 1.0),
        "p": nrm(ks[1], (DEPTH, BATCH, SEQ, PLE_DIM), 1.0),
        "w_in": nrm(ks[2], (DEPTH, D_MODEL, IN_WIDTH), D_MODEL ** -0.5),
        "w_pool": nrm(ks[3], (DEPTH, N_POOL_GROUPS, POOL_GROUP, POOL_GROUP), POOL_GROUP ** -0.5),
        "pool_scale": 1.0 + nrm(ks[4], (DEPTH, POOL_WIDTH), 0.1),
        "attn_sinks": nrm(ks[5], (DEPTH, N_Q_HEADS), 0.5),
        "w_out": nrm(ks[6], (DEPTH, MIX_WIDTH, D_MODEL), MIX_WIDTH ** -0.5 * DEEPNORM_BETA),
        "ln1_g": 1.0 + nrm(ks[7], (DEPTH, D_MODEL), 0.05),
        "ln1_b": nrm(ks[8], (DEPTH, D_MODEL), 0.02),
        "w_ff1": nrm(ks[9], (DEPTH, D_MODEL, D_FF), D_MODEL ** -0.5),
        "w_ff2": nrm(ks[10], (DEPTH, D_FF, D_MODEL), D_FF ** -0.5 * DEEPNORM_BETA),
        "ln2_g": 1.0 + nrm(ks[11], (DEPTH, D_MODEL), 0.05),
        "ln2_b": nrm(ks[12], (DEPTH, D_MODEL), 0.02),
        "w_ple": nrm(ks[13], (DEPTH, PLE_DIM, D_MODEL), PLE_DIM ** -0.5),
        "w_ple_gate": nrm(ks[14], (DEPTH, D_MODEL, D_MODEL), D_MODEL ** -0.5),
        "b_ple_gate": nrm(ks[15], (DEPTH, D_MODEL), 0.02),
    }


def reference(x, p, w_in, w_pool, pool_scale, attn_sinks, w_out, ln1_g, ln1_b,
              w_ff1, w_ff2, ln2_g, ln2_b, w_ple, w_ple_gate, b_ple_gate):
    B, S, _ = x.shape
    h = x
    for i in range(DEPTH):
        z = h @ w_in[i]
        u_pool = z[..., :POOL_WIDTH]
        q = z[..., POOL_WIDTH:POOL_WIDTH + ATTN_WIDTH].reshape(B, S, N_Q_HEADS, HEAD_DIM)
        k = z[..., POOL_WIDTH + ATTN_WIDTH:POOL_WIDTH + ATTN_WIDTH + KV_WIDTH].reshape(B, S, N_KV_HEADS, HEAD_DIM)
        v = z[..., POOL_WIDTH + ATTN_WIDTH + KV_WIDTH:].reshape(B, S, N_KV_HEADS, HEAD_DIM)

        y_pool = pool_mixer(u_pool, w_pool[i], pool_scale[i])
        y_attn = sliding_window_attention(q, k, v, attn_sinks[i])
        mix = jnp.concatenate([y_pool, y_attn], axis=-1) @ w_out[i]
        h = layer_norm(DEEPNORM_ALPHA * h + mix, ln1_g[i], ln1_b[i])

        ff = jnp.square(jax.nn.relu(h @ w_ff1[i])) @ w_ff2[i]
        h = layer_norm(DEEPNORM_ALPHA * h + ff, ln2_g[i], ln2_b[i])

        gate = jax.nn.sigmoid(h @ w_ple_gate[i] + b_ple_gate[i])
        h = h + gate * (p[i] @ w_ple[i])
    return h
```

```python
import functools
import math

import jax
import jax.numpy as jnp
from jax import lax
from jax.experimental import pallas as pl
from jax.experimental.pallas import tpu as pltpu

D_MODEL = 1024
PLE_DIM = 256
POOL_WIDTH = 512
POOL_WINDOWS = (2, 4, 8, 16)
POOL_GROUP = 128
HEAD_DIM = 64
ATTN_WIDTH = 512
N_Q_HEADS = 8
N_KV_HEADS = 2
GQA_GROUP = N_Q_HEADS // N_KV_HEADS
KV_WIDTH = N_KV_HEADS * HEAD_DIM
IN_WIDTH = POOL_WIDTH + ATTN_WIDTH + 2 * KV_WIDTH
BLOCK = 128
D_FF = 4 * D_MODEL
LN_EPS = 1e-5
NEG_INF = -1e30

POOL_HALO = 16
MIX_TILE = 512
FFN_TILE = 512
FF_CHUNK = 1024
VMEM_LIMIT_BYTES = 56 * 1024 * 1024

PLAIN_HEADS = tuple(h for h in range(N_Q_HEADS) if h % 2 == h // GQA_GROUP)
SWAP_HEADS = tuple(h for h in range(N_Q_HEADS) if h % 2 != h // GQA_GROUP)


def _layer_norm(y, g, b):
    mu = jnp.mean(y, axis=-1, keepdims=True)
    yc = y - mu
    var = jnp.mean(yc * yc, axis=-1, keepdims=True)
    return yc * lax.rsqrt(var + LN_EPS) * g + b


def _const_spec(shape):
    return pl.BlockSpec(shape, lambda *_: (0,) * len(shape), pipeline_mode=pl.Buffered(1))


def _mixer_kernel(sinks_ref, x_ref, w_in_ref, w_pool_ref, pool_scale_ref, w_out_ref, g_ref, b_ref,
                  o_ref, z_ref, mix_ref, ucarry_ref, kvprev_ref, *, alpha):
    tile = pl.program_id(1)
    ts = x_ref.shape[0]

    @pl.when(tile == 0)
    def _():
        ucarry_ref[...] = jnp.zeros_like(ucarry_ref)
        kvprev_ref[...] = jnp.zeros_like(kvprev_ref)

    x = x_ref[...]
    z_ref[...] = jnp.dot(x.astype(jnp.bfloat16), w_in_ref[...], preferred_element_type=jnp.float32)

    pos = tile * ts + lax.broadcasted_iota(jnp.int32, (ts, POOL_GROUP), 0)
    for g, w in enumerate(POOL_WINDOWS):
        cols = slice(g * POOL_GROUP, (g + 1) * POOL_GROUP)
        u = z_ref[:, cols]
        s = jnp.concatenate([ucarry_ref[:, cols], u], axis=0)
        shift = 1
        while shift < w:
            s = s + pltpu.roll(s, shift, axis=0)
            shift *= 2
        count = jnp.minimum(pos + 1, w).astype(jnp.float32)
        d = s[POOL_HALO:] / count - u
        y = jnp.dot(d.astype(jnp.bfloat16), w_pool_ref[g], preferred_element_type=jnp.float32)
        mix_ref[:, cols] = (y * pool_scale_ref[:, cols]).astype(jnp.bfloat16)
    ucarry_ref[...] = z_ref[ts - POOL_HALO:, :POOL_WIDTH]

    qi = lax.broadcasted_iota(jnp.int32, (BLOCK, 2 * BLOCK), 0)
    kj = lax.broadcasted_iota(jnp.int32, (BLOCK, 2 * BLOCK), 1)
    dist = qi + BLOCK - kj
    in_band = (dist >= 0) & (dist < BLOCK)
    dist_f = dist.astype(jnp.float32)
    low_half = lax.broadcasted_iota(jnp.int32, (BLOCK, 2 * HEAD_DIM), 1) < HEAD_DIM

    def attn_block(blk, carry):
        r0 = pl.multiple_of(blk * BLOCK, BLOCK)
        rows = pl.ds(r0, BLOCK)
        valid = in_band & ((kj >= BLOCK) | (tile * (ts // BLOCK) + blk > 0))

        k_f = z_ref[rows, POOL_WIDTH + ATTN_WIDTH:POOL_WIDTH + ATTN_WIDTH + KV_WIDTH]
        v_f = z_ref[rows, POOL_WIDTH + ATTN_WIDTH + KV_WIDTH:]
        cur = jnp.concatenate(
            [k_f, v_f, pltpu.roll(k_f, HEAD_DIM, axis=1), pltpu.roll(v_f, HEAD_DIM, axis=1)],
            axis=1).astype(jnp.bfloat16)
        band = jnp.concatenate([kvprev_ref[...], cur], axis=0)
        kvprev_ref[...] = cur

        outs = {}
        for heads, off in ((PLAIN_HEADS, 0), (SWAP_HEADS, 2 * KV_WIDTH)):
            k_band = band[:, off:off + KV_WIDTH]
            v_band = band[:, off + KV_WIDTH:off + 2 * KV_WIDTH]
            q_parts = []
            for h in heads:
                pair = POOL_WIDTH + (h // 2) * 2 * HEAD_DIM
                q = z_ref[rows, pair:pair + 2 * HEAD_DIM] * (1.0 / math.sqrt(HEAD_DIM))
                keep = low_half if h % 2 == 0 else jnp.logical_not(low_half)
                q_parts.append(jnp.where(keep, q, 0.0).astype(jnp.bfloat16))
            q_stack = jnp.concatenate(q_parts, axis=0)
            scores = lax.dot_general(q_stack, k_band, (((1,), (1,)), ((), ())),
                                     preferred_element_type=jnp.float32)
            p_parts, inv_parts = [], []
            for j, h in enumerate(heads):
                slope = 2.0 ** (-8.0 * (h + 1) / N_Q_HEADS)
                sink = sinks_ref[h]
                sc = jnp.where(valid, scores[j * BLOCK:(j + 1) * BLOCK] - slope * dist_f, NEG_INF)
                m = jnp.maximum(jnp.max(sc, axis=-1, keepdims=True), sink)
                e = jnp.exp(sc - m)
                denom = jnp.sum(e, axis=-1, keepdims=True) + jnp.exp(sink - m)
                p_parts.append(e.astype(jnp.bfloat16))
                inv_parts.append(1.0 / denom)
            pv = jnp.dot(jnp.concatenate(p_parts, axis=0), v_band,
                         preferred_element_type=jnp.float32)
            for j, h in enumerate(heads):
                outs[h] = pv[j * BLOCK:(j + 1) * BLOCK] * inv_parts[j]
        for pair in range(N_Q_HEADS // 2):
            col = POOL_WIDTH + pair * 2 * HEAD_DIM
            merged = jnp.where(low_half, outs[2 * pair], outs[2 * pair + 1])
            mix_ref[rows, col:col + 2 * HEAD_DIM] = merged.astype(jnp.bfloat16)
        return carry

    lax.fori_loop(0, ts // BLOCK, attn_block, 0)

    mix = jnp.dot(mix_ref[...], w_out_ref[...], preferred_element_type=jnp.float32)
    o_ref[...] = _layer_norm(alpha * x + mix, g_ref[...], b_ref[...])


def _ffn_kernel(h_ref, p_ref, w1_ref, w2_ref, g_ref, b_ref, wple_ref, wgate_ref, bgate_ref, o_ref, *,
                alpha):
    h = h_ref[...]
    hb = h.astype(jnp.bfloat16)
    acc = jnp.zeros(h.shape, jnp.float32)
    for c in range(D_FF // FF_CHUNK):
        cols = slice(c * FF_CHUNK, (c + 1) * FF_CHUNK)
        hid = jnp.maximum(jnp.dot(hb, w1_ref[:, cols], preferred_element_type=jnp.float32), 0.0)
        acc = acc + jnp.dot((hid * hid).astype(jnp.bfloat16), w2_ref[cols, :],
                            preferred_element_type=jnp.float32)
    h2 = _layer_norm(alpha * h + acc, g_ref[...], b_ref[...])
    gate = jax.nn.sigmoid(
        jnp.dot(h2.astype(jnp.bfloat16), wgate_ref[...], preferred_element_type=jnp.float32)
        + bgate_ref[...])
    ple = jnp.dot(p_ref[...].astype(jnp.bfloat16), wple_ref[...], preferred_element_type=jnp.float32)
    o_ref[...] = h2 + gate * ple


def _mixer_call(x, sinks, w_in, w_pool, pool_scale, w_out, ln_g, ln_b, alpha):
    batch, seq, d = x.shape
    assert d == D_MODEL and seq % MIX_TILE == 0 and MIX_TILE % BLOCK == 0
    grid_spec = pltpu.PrefetchScalarGridSpec(
        num_scalar_prefetch=1,
        grid=(batch, seq // MIX_TILE),
        in_specs=[
            pl.BlockSpec((None, MIX_TILE, d), lambda b, i, *_: (b, i, 0)),
            _const_spec((d, IN_WIDTH)),
            _const_spec((len(POOL_WINDOWS), POOL_GROUP, POOL_GROUP)),
            _const_spec((1, POOL_WIDTH)),
            _const_spec((POOL_WIDTH + ATTN_WIDTH, d)),
            _const_spec((1, d)),
            _const_spec((1, d)),
        ],
        out_specs=pl.BlockSpec((None, MIX_TILE, d), lambda b, i, *_: (b, i, 0)),
        scratch_shapes=[
            pltpu.VMEM((MIX_TILE, IN_WIDTH), jnp.float32),
            pltpu.VMEM((MIX_TILE, POOL_WIDTH + ATTN_WIDTH), jnp.bfloat16),
            pltpu.VMEM((POOL_HALO, POOL_WIDTH), jnp.float32),
            pltpu.VMEM((BLOCK, 4 * KV_WIDTH), jnp.bfloat16),
        ],
    )
    return pl.pallas_call(
        functools.partial(_mixer_kernel, alpha=alpha),
        grid_spec=grid_spec,
        out_shape=jax.ShapeDtypeStruct(x.shape, jnp.float32),
        compiler_params=pltpu.CompilerParams(
            dimension_semantics=("arbitrary", "arbitrary"), vmem_limit_bytes=VMEM_LIMIT_BYTES),
        name="mixer",
    )(sinks, x, w_in, w_pool, pool_scale, w_out, ln_g, ln_b)


def _ffn_call(h, p, w1, w2, ln_g, ln_b, w_ple, w_gate, b_gate, alpha):
    batch, seq, d = h.shape
    assert seq % FFN_TILE == 0
    return pl.pallas_call(
        functools.partial(_ffn_kernel, alpha=alpha),
        grid=(batch, seq // FFN_TILE),
        in_specs=[
            pl.BlockSpec((None, FFN_TILE, d), lambda b, i: (b, i, 0)),
            pl.BlockSpec((None, FFN_TILE, PLE_DIM), lambda b, i: (b, i, 0)),
            _const_spec((d, D_FF)),
            _const_spec((D_FF, d)),
            _const_spec((1, d)),
            _const_spec((1, d)),
            _const_spec((PLE_DIM, d)),
            _const_spec((d, d)),
            _const_spec((1, d)),
        ],
        out_specs=pl.BlockSpec((None, FFN_TILE, d), lambda b, i: (b, i, 0)),
        out_shape=jax.ShapeDtypeStruct(h.shape, jnp.float32),
        compiler_params=pltpu.CompilerParams(
            dimension_semantics=("arbitrary", "arbitrary"), vmem_limit_bytes=VMEM_LIMIT_BYTES),
        name="ffn",
    )(h, p, w1, w2, ln_g, ln_b, w_ple, w_gate, b_gate)


def kernel(x, p, w_in, w_pool, pool_scale, attn_sinks, w_out, ln1_g, ln1_b, w_ff1, w_ff2, ln2_g, ln2_b,
           w_ple, w_ple_gate, b_ple_gate):
    depth = w_in.shape[0]
    alpha = (2 * depth) ** 0.25
    bf = jnp.bfloat16
    row = lambda v: v.reshape(1, -1)
    h = x
    for i in range(depth):
        h = _mixer_call(h, attn_sinks[i], w_in[i].astype(bf), w_pool[i].astype(bf), row(pool_scale[i]),
                        w_out[i].astype(bf), row(ln1_g[i]), row(ln1_b[i]), alpha)
        h = _ffn_call(h, p[i], w_ff1[i].astype(bf), w_ff2[i].astype(bf), row(ln2_g[i]), row(ln2_b[i]),
                      w_ple[i].astype(bf), w_ple_gate[i].astype(bf), row(b_ple_gate[i]), alpha)
    return h
```

```python
import functools
import math

import jax
import jax.numpy as jnp
from jax import lax
from jax.experimental import pallas as pl
from jax.experimental.pallas import tpu as pltpu

D_MODEL = 1024
PLE_DIM = 256
POOL_WIDTH = 512
POOL_WINDOWS = (2, 4, 8, 16)
POOL_GROUP = 128
HEAD_DIM = 64
ATTN_WIDTH = 512
N_Q_HEADS = 8
N_KV_HEADS = 2
GQA_GROUP = N_Q_HEADS // N_KV_HEADS
KV_WIDTH = N_KV_HEADS * HEAD_DIM
IN_WIDTH = POOL_WIDTH + ATTN_WIDTH + 2 * KV_WIDTH
BLOCK = 128
D_FF = 4 * D_MODEL
LN_EPS = 1e-5
NEG_INF = -1e30

POOL_HALO = 16
MIX_TILE = 512
FFN_TILE = 512
FF_CHUNK = 1024
VMEM_LIMIT_BYTES = 56 * 1024 * 1024

PLAIN_HEADS = tuple(h for h in range(N_Q_HEADS) if h % 2 == h // GQA_GROUP)
SWAP_HEADS = tuple(h for h in range(N_Q_HEADS) if h % 2 != h // GQA_GROUP)


def _layer_norm(y, g, b):
    mu = jnp.mean(y, axis=-1, keepdims=True)
    yc = y - mu
    var = jnp.mean(yc * yc, axis=-1, keepdims=True)
    return yc * lax.rsqrt(var + LN_EPS) * g + b


def _const_spec(shape):
    return pl.BlockSpec(shape, lambda *_: (0,) * len(shape), pipeline_mode=pl.Buffered(1))


def _mixer_kernel(sinks_ref, x_ref, w_in_ref, w_pool_ref, pool_scale_ref, w_out_ref, g_ref, b_ref,
                  o_ref, z_ref, mix_ref, halo_ref, kvprev_ref, logit_ref, *, alpha):
    tile = pl.program_id(1)
    ts = x_ref.shape[0]
    slot = tile % 2

    @pl.when(tile == 0)
    def _():
        halo_ref[0] = jnp.zeros(halo_ref.shape[1:], halo_ref.dtype)
        kvprev_ref[0] = jnp.zeros(kvprev_ref.shape[1:], kvprev_ref.dtype)

    x = x_ref[...]
    z_ref[...] = jnp.dot(x.astype(jnp.bfloat16), w_in_ref[...], preferred_element_type=jnp.float32)

    pos = tile * ts + lax.broadcasted_iota(jnp.int32, (ts, POOL_GROUP), 0)
    halo = halo_ref[slot]
    for g, w in enumerate(POOL_WINDOWS):
        cols = slice(g * POOL_GROUP, (g + 1) * POOL_GROUP)
        u = z_ref[:, cols]
        s = jnp.concatenate([halo[:, cols], u], axis=0)
        shift = 1
        while shift < w:
            s = s + pltpu.roll(s, shift, axis=0)
            shift *= 2
        count = jnp.minimum(pos + 1, w).astype(jnp.float32)
        d = s[POOL_HALO:] / count - u
        y = jnp.dot(d.astype(jnp.bfloat16), w_pool_ref[g], preferred_element_type=jnp.float32)
        mix_ref[:, cols] = (y * pool_scale_ref[:, cols]).astype(jnp.bfloat16)
    halo_ref[1 - slot] = z_ref[ts - POOL_HALO:, :POOL_WIDTH]

    qi = lax.broadcasted_iota(jnp.int32, (BLOCK, 2 * BLOCK), 0)
    kj = lax.broadcasted_iota(jnp.int32, (BLOCK, 2 * BLOCK), 1)
    dist = qi + BLOCK - kj
    in_band = (dist >= 0) & (dist < BLOCK)
    low_half = lax.broadcasted_iota(jnp.int32, (BLOCK, 2 * HEAD_DIM), 1) < HEAD_DIM

    @pl.when((pl.program_id(0) == 0) & (tile == 0))
    def _():
        dist_f = dist.astype(jnp.float32)
        for first in (0, 1):
            ok = in_band & (kj >= BLOCK) if first == 0 else in_band
            for h in range(N_Q_HEADS):
                slope = 2.0 ** (-8.0 * (h + 1) / N_Q_HEADS)
                masked = jnp.where(kj == qi, sinks_ref[h:h + 1, :], NEG_INF)
                logit_ref[first, h] = jnp.where(ok, -slope * dist_f, masked)

    def attn_block(blk, prev):
        rows = slice(blk * BLOCK, (blk + 1) * BLOCK)
        table = jnp.minimum(tile * (ts // BLOCK) + blk, 1)
        valid = in_band & (kj >= BLOCK * (1 - table))

        k_f = z_ref[rows, POOL_WIDTH + ATTN_WIDTH:POOL_WIDTH + ATTN_WIDTH + KV_WIDTH]
        v_f = z_ref[rows, POOL_WIDTH + ATTN_WIDTH + KV_WIDTH:]
        cur = jnp.concatenate(
            [k_f, v_f, pltpu.roll(k_f, HEAD_DIM, axis=1), pltpu.roll(v_f, HEAD_DIM, axis=1)],
            axis=1).astype(jnp.bfloat16)
        band = jnp.concatenate([prev, cur], axis=0)

        outs = {}
        for heads, off in ((PLAIN_HEADS, 0), (SWAP_HEADS, 2 * KV_WIDTH)):
            k_band = band[:, off:off + KV_WIDTH]
            v_band = band[:, off + KV_WIDTH:off + 2 * KV_WIDTH]
            q_parts = []
            for h in heads:
                pair = POOL_WIDTH + (h // 2) * 2 * HEAD_DIM
                q = z_ref[rows, pair:pair + 2 * HEAD_DIM] * (1.0 / math.sqrt(HEAD_DIM))
                keep = low_half if h % 2 == 0 else jnp.logical_not(low_half)
                q_parts.append(jnp.where(keep, q, 0.0).astype(jnp.bfloat16))
            q_stack = jnp.concatenate(q_parts, axis=0)
            scores = lax.dot_general(q_stack, k_band, (((1,), (1,)), ((), ())),
                                     preferred_element_type=jnp.float32)
            p_parts, inv_parts = [], []
            for j, h in enumerate(heads):
                sc = jnp.where(valid, scores[j * BLOCK:(j + 1) * BLOCK], 0.0) + logit_ref[table, h]
                e = jnp.exp(sc - jnp.max(sc, axis=-1, keepdims=True))
                inv_parts.append(1.0 / jnp.sum(e, axis=-1, keepdims=True))
                p_parts.append(jnp.where(valid, e, 0.0).astype(jnp.bfloat16))
            pv = jnp.dot(jnp.concatenate(p_parts, axis=0), v_band,
                         preferred_element_type=jnp.float32)
            for j, h in enumerate(heads):
                outs[h] = pv[j * BLOCK:(j + 1) * BLOCK] * inv_parts[j]
        for pair in range(N_Q_HEADS // 2):
            col = POOL_WIDTH + pair * 2 * HEAD_DIM
            merged = jnp.where(low_half, outs[2 * pair], outs[2 * pair + 1])
            mix_ref[rows, col:col + 2 * HEAD_DIM] = merged.astype(jnp.bfloat16)
        return cur

    prev = kvprev_ref[slot]
    for blk in range(ts // BLOCK):
        prev = attn_block(blk, prev)
    kvprev_ref[1 - slot] = prev

    mix = jnp.dot(mix_ref[...], w_out_ref[...], preferred_element_type=jnp.float32)
    o_ref[...] = _layer_norm(alpha * x + mix, g_ref[...], b_ref[...])


def _ffn_kernel(h_ref, p_ref, w1_ref, w2_ref, g_ref, b_ref, wple_ref, wgate_ref, bgate_ref, o_ref, *,
                alpha):
    h = h_ref[...]
    hb = h.astype(jnp.bfloat16)
    acc = jnp.zeros(h.shape, jnp.float32)
    for c in range(D_FF // FF_CHUNK):
        cols = slice(c * FF_CHUNK, (c + 1) * FF_CHUNK)
        hid = jnp.maximum(jnp.dot(hb, w1_ref[:, cols], preferred_element_type=jnp.float32), 0.0)
        acc = acc + jnp.dot((hid * hid).astype(jnp.bfloat16), w2_ref[cols, :],
                            preferred_element_type=jnp.float32)
    h2 = _layer_norm(alpha * h + acc, g_ref[...], b_ref[...])
    gate = jax.nn.sigmoid(
        jnp.dot(h2.astype(jnp.bfloat16), wgate_ref[...], preferred_element_type=jnp.float32)
        + bgate_ref[...])
    ple = jnp.dot(p_ref[...].astype(jnp.bfloat16), wple_ref[...], preferred_element_type=jnp.float32)
    o_ref[...] = h2 + gate * ple


def _mixer_call(x, sinks, w_in, w_pool, pool_scale, w_out, ln_g, ln_b, alpha):
    batch, seq, d = x.shape
    assert d == D_MODEL and seq % MIX_TILE == 0 and MIX_TILE % BLOCK == 0
    grid_spec = pl.GridSpec(
        grid=(batch, seq // MIX_TILE),
        in_specs=[
            _const_spec((N_Q_HEADS, 2 * BLOCK)),
            pl.BlockSpec((None, MIX_TILE, d), lambda b, i: (b, i, 0)),
            _const_spec((d, IN_WIDTH)),
            _const_spec((len(POOL_WINDOWS), POOL_GROUP, POOL_GROUP)),
            _const_spec((1, POOL_WIDTH)),
            _const_spec((POOL_WIDTH + ATTN_WIDTH, d)),
            _const_spec((1, d)),
            _const_spec((1, d)),
        ],
        out_specs=pl.BlockSpec((None, MIX_TILE, d), lambda b, i: (b, i, 0)),
        scratch_shapes=[
            pltpu.VMEM((MIX_TILE, IN_WIDTH), jnp.float32),
            pltpu.VMEM((MIX_TILE, POOL_WIDTH + ATTN_WIDTH), jnp.bfloat16),
            pltpu.VMEM((2, POOL_HALO, POOL_WIDTH), jnp.float32),
            pltpu.VMEM((2, BLOCK, 4 * KV_WIDTH), jnp.bfloat16),
            pltpu.VMEM((2, N_Q_HEADS, BLOCK, 2 * BLOCK), jnp.float32),
        ],
    )
    return pl.pallas_call(
        functools.partial(_mixer_kernel, alpha=alpha),
        grid_spec=grid_spec,
        out_shape=jax.ShapeDtypeStruct(x.shape, jnp.float32),
        compiler_params=pltpu.CompilerParams(
            dimension_semantics=("arbitrary", "arbitrary"), vmem_limit_bytes=VMEM_LIMIT_BYTES),
        name="mixer",
    )(jnp.broadcast_to(sinks[:, None], (N_Q_HEADS, 2 * BLOCK)), x, w_in, w_pool, pool_scale, w_out, ln_g,
      ln_b)


def _ffn_call(h, p, w1, w2, ln_g, ln_b, w_ple, w_gate, b_gate, alpha):
    batch, seq, d = h.shape
    assert seq % FFN_TILE == 0
    return pl.pallas_call(
        functools.partial(_ffn_kernel, alpha=alpha),
        grid=(batch, seq // FFN_TILE),
        in_specs=[
            pl.BlockSpec((None, FFN_TILE, d), lambda b, i: (b, i, 0)),
            pl.BlockSpec((None, FFN_TILE, PLE_DIM), lambda b, i: (b, i, 0)),
            _const_spec((d, D_FF)),
            _const_spec((D_FF, d)),
            _const_spec((1, d)),
            _const_spec((1, d)),
            _const_spec((PLE_DIM, d)),
            _const_spec((d, d)),
            _const_spec((1, d)),
        ],
        out_specs=pl.BlockSpec((None, FFN_TILE, d), lambda b, i: (b, i, 0)),
        out_shape=jax.ShapeDtypeStruct(h.shape, jnp.float32),
        compiler_params=pltpu.CompilerParams(
            dimension_semantics=("arbitrary", "arbitrary"), vmem_limit_bytes=VMEM_LIMIT_BYTES),
        name="ffn",
    )(h, p, w1, w2, ln_g, ln_b, w_ple, w_gate, b_gate)


def kernel(x, p, w_in, w_pool, pool_scale, attn_sinks, w_out, ln1_g, ln1_b, w_ff1, w_ff2, ln2_g, ln2_b,
           w_ple, w_ple_gate, b_ple_gate):
    depth = w_in.shape[0]
    alpha = (2 * depth) ** 0.25
    bf = jnp.bfloat16
    row = lambda v: v.reshape(1, -1)
    h = x
    for i in range(depth):
        h = _mixer_call(h, attn_sinks[i], w_in[i].astype(bf), w_pool[i].astype(bf), row(pool_scale[i]),
                        w_out[i].astype(bf), row(ln1_g[i]), row(ln1_b[i]), alpha)
        h = _ffn_call(h, p[i], w_ff1[i].astype(bf), w_ff2[i].astype(bf), row(ln2_g[i]), row(ln2_b[i]),
                      w_ple[i].astype(bf), w_ple_gate[i].astype(bf), row(b_ple_gate[i]), alpha)
    return h
```

```python
import functools
import math

import jax
import jax.numpy as jnp
from jax import lax
from jax.experimental import pallas as pl
from jax.experimental.pallas import tpu as pltpu

D_MODEL = 1024
PLE_DIM = 256
POOL_WIDTH = 512
POOL_WINDOWS = (2, 4, 8, 16)
POOL_GROUP = 128
HEAD_DIM = 64
ATTN_WIDTH = 512
N_Q_HEADS = 8
N_KV_HEADS = 2
GQA_GROUP = N_Q_HEADS // N_KV_HEADS
KV_WIDTH = N_KV_HEADS * HEAD_DIM
IN_WIDTH = POOL_WIDTH + ATTN_WIDTH + 2 * KV_WIDTH
BLOCK = 128
D_FF = 4 * D_MODEL
LN_EPS = 1e-5
NEG_INF = -1e30

POOL_HALO = 16
MIX_TILE = 1024
MIX_SUB = 256
FFN_TILE = 512
FF_CHUNK = 1024
VMEM_LIMIT_BYTES = 56 * 1024 * 1024

PLAIN_HEADS = tuple(h for h in range(N_Q_HEADS) if h % 2 == h // GQA_GROUP)
SWAP_HEADS = tuple(h for h in range(N_Q_HEADS) if h % 2 != h // GQA_GROUP)


def _layer_norm(y, g, b):
    mu = jnp.mean(y, axis=-1, keepdims=True)
    yc = y - mu
    var = jnp.mean(yc * yc, axis=-1, keepdims=True)
    return yc * lax.rsqrt(var + LN_EPS) * g + b


def _const_spec(shape):
    return pl.BlockSpec(shape, lambda *_: (0,) * len(shape), pipeline_mode=pl.Buffered(1))


def _mixer_kernel(sinks_ref, x_ref, w_in_ref, w_pool_ref, pool_scale_ref, w_out_ref, g_ref, b_ref,
                  o_ref, z_ref, mix_ref, halo_ref, kvprev_ref, logit_ref, *, alpha):
    tile = pl.program_id(1)
    ts = x_ref.shape[0]
    slot = tile % 2

    @pl.when(tile == 0)
    def _():
        halo_ref[0] = jnp.zeros(halo_ref.shape[1:], halo_ref.dtype)
        kvprev_ref[0] = jnp.zeros(kvprev_ref.shape[1:], kvprev_ref.dtype)

    qi = lax.broadcasted_iota(jnp.int32, (BLOCK, 2 * BLOCK), 0)
    kj = lax.broadcasted_iota(jnp.int32, (BLOCK, 2 * BLOCK), 1)
    dist = qi + BLOCK - kj
    in_band = (dist >= 0) & (dist < BLOCK)
    low_half = lax.broadcasted_iota(jnp.int32, (BLOCK, 2 * HEAD_DIM), 1) < HEAD_DIM

    @pl.when((pl.program_id(0) == 0) & (tile == 0))
    def _():
        dist_f = dist.astype(jnp.float32)
        for first in (0, 1):
            ok = in_band & (kj >= BLOCK) if first == 0 else in_band
            for h in range(N_Q_HEADS):
                slope = 2.0 ** (-8.0 * (h + 1) / N_Q_HEADS)
                masked = jnp.where(kj == qi, sinks_ref[h:h + 1, :], NEG_INF)
                logit_ref[first, h] = jnp.where(ok, -slope * dist_f, masked)

    def attn_block(blk, prev):
        rows = slice(blk * BLOCK, (blk + 1) * BLOCK)
        table = jnp.minimum(tile * (ts // BLOCK) + blk, 1)
        valid = in_band & (kj >= BLOCK * (1 - table))

        k_f = z_ref[rows, POOL_WIDTH + ATTN_WIDTH:POOL_WIDTH + ATTN_WIDTH + KV_WIDTH]
        v_f = z_ref[rows, POOL_WIDTH + ATTN_WIDTH + KV_WIDTH:]
        cur = jnp.concatenate(
            [k_f, v_f, pltpu.roll(k_f, HEAD_DIM, axis=1), pltpu.roll(v_f, HEAD_DIM, axis=1)],
            axis=1).astype(jnp.bfloat16)
        band = jnp.concatenate([prev, cur], axis=0)

        outs = {}
        for heads, off in ((PLAIN_HEADS, 0), (SWAP_HEADS, 2 * KV_WIDTH)):
            k_band = band[:, off:off + KV_WIDTH]
            v_band = band[:, off + KV_WIDTH:off + 2 * KV_WIDTH]
            q_parts = []
            for h in heads:
                pair = POOL_WIDTH + (h // 2) * 2 * HEAD_DIM
                q = z_ref[rows, pair:pair + 2 * HEAD_DIM] * (1.0 / math.sqrt(HEAD_DIM))
                keep = low_half if h % 2 == 0 else jnp.logical_not(low_half)
                q_parts.append(jnp.where(keep, q, 0.0).astype(jnp.bfloat16))
            q_stack = jnp.concatenate(q_parts, axis=0)
            scores = lax.dot_general(q_stack, k_band, (((1,), (1,)), ((), ())),
                                     preferred_element_type=jnp.float32)
            p_parts, inv_parts = [], []
            for j, h in enumerate(heads):
                sc = jnp.where(valid, scores[j * BLOCK:(j + 1) * BLOCK], 0.0) + logit_ref[table, h]
                e = jnp.exp(sc - jnp.max(sc, axis=-1, keepdims=True))
                inv_parts.append(1.0 / jnp.sum(e, axis=-1, keepdims=True))
                p_parts.append(jnp.where(valid, e, 0.0).astype(jnp.bfloat16))
            pv = jnp.dot(jnp.concatenate(p_parts, axis=0), v_band,
                         preferred_element_type=jnp.float32)
            for j, h in enumerate(heads):
                outs[h] = pv[j * BLOCK:(j + 1) * BLOCK] * inv_parts[j]
        for pair in range(N_Q_HEADS // 2):
            col = POOL_WIDTH + pair * 2 * HEAD_DIM
            merged = jnp.where(low_half, outs[2 * pair], outs[2 * pair + 1])
            mix_ref[rows, col:col + 2 * HEAD_DIM] = merged.astype(jnp.bfloat16)
        return cur

    halo = halo_ref[slot]
    prev = kvprev_ref[slot]
    for sub in range(ts // MIX_SUB):
        r0 = sub * MIX_SUB
        rows = slice(r0, r0 + MIX_SUB)
        z_ref[rows, :] = jnp.dot(x_ref[rows, :].astype(jnp.bfloat16), w_in_ref[...],
                                 preferred_element_type=jnp.float32)

        pos = tile * ts + r0 + lax.broadcasted_iota(jnp.int32, (MIX_SUB, POOL_GROUP), 0)
        for g, w in enumerate(POOL_WINDOWS):
            cols = slice(g * POOL_GROUP, (g + 1) * POOL_GROUP)
            u = z_ref[rows, cols]
            s = jnp.concatenate([halo[:, cols], u], axis=0)
            shift = 1
            while shift < w:
                s = s + pltpu.roll(s, shift, axis=0)
                shift *= 2
            count = jnp.minimum(pos + 1, w).astype(jnp.float32)
            d = s[POOL_HALO:] / count - u
            y = jnp.dot(d.astype(jnp.bfloat16), w_pool_ref[g], preferred_element_type=jnp.float32)
            mix_ref[rows, cols] = (y * pool_scale_ref[:, cols]).astype(jnp.bfloat16)
        halo = z_ref[r0 + MIX_SUB - POOL_HALO:r0 + MIX_SUB, :POOL_WIDTH]

        for blk in range(r0 // BLOCK, (r0 + MIX_SUB) // BLOCK):
            prev = attn_block(blk, prev)

        mix = jnp.dot(mix_ref[rows, :], w_out_ref[...], preferred_element_type=jnp.float32)
        o_ref[rows, :] = _layer_norm(alpha * x_ref[rows, :] + mix, g_ref[...], b_ref[...])
    halo_ref[1 - slot] = halo
    kvprev_ref[1 - slot] = prev


def _ffn_kernel(h_ref, p_ref, w1_ref, w2_ref, g_ref, b_ref, wple_ref, wgate_ref, bgate_ref, o_ref, *,
                alpha):
    h = h_ref[...]
    hb = h.astype(jnp.bfloat16)
    acc = jnp.zeros(h.shape, jnp.float32)
    for c in range(D_FF // FF_CHUNK):
        cols = slice(c * FF_CHUNK, (c + 1) * FF_CHUNK)
        hid = jnp.maximum(jnp.dot(hb, w1_ref[:, cols], preferred_element_type=jnp.float32), 0.0)
        acc = acc + jnp.dot((hid * hid).astype(jnp.bfloat16), w2_ref[cols, :],
                            preferred_element_type=jnp.float32)
    h2 = _layer_norm(alpha * h + acc, g_ref[...], b_ref[...])
    gate = jax.nn.sigmoid(
        jnp.dot(h2.astype(jnp.bfloat16), wgate_ref[...], preferred_element_type=jnp.float32)
        + bgate_ref[...])
    ple = jnp.dot(p_ref[...].astype(jnp.bfloat16), wple_ref[...], preferred_element_type=jnp.float32)
    o_ref[...] = h2 + gate * ple


def _mixer_call(x, sinks, w_in, w_pool, pool_scale, w_out, ln_g, ln_b, alpha):
    batch, seq, d = x.shape
    assert d == D_MODEL and seq % MIX_TILE == 0 and MIX_TILE % MIX_SUB == 0 and MIX_SUB % BLOCK == 0
    grid_spec = pl.GridSpec(
        grid=(batch, seq // MIX_TILE),
        in_specs=[
            _const_spec((N_Q_HEADS, 2 * BLOCK)),
            pl.BlockSpec((None, MIX_TILE, d), lambda b, i: (b, i, 0)),
            _const_spec((d, IN_WIDTH)),
            _const_spec((len(POOL_WINDOWS), POOL_GROUP, POOL_GROUP)),
            _const_spec((1, POOL_WIDTH)),
            _const_spec((POOL_WIDTH + ATTN_WIDTH, d)),
            _const_spec((1, d)),
            _const_spec((1, d)),
        ],
        out_specs=pl.BlockSpec((None, MIX_TILE, d), lambda b, i: (b, i, 0)),
        scratch_shapes=[
            pltpu.VMEM((MIX_TILE, IN_WIDTH), jnp.float32),
            pltpu.VMEM((MIX_TILE, POOL_WIDTH + ATTN_WIDTH), jnp.bfloat16),
            pltpu.VMEM((2, POOL_HALO, POOL_WIDTH), jnp.float32),
            pltpu.VMEM((2, BLOCK, 4 * KV_WIDTH), jnp.bfloat16),
            pltpu.VMEM((2, N_Q_HEADS, BLOCK, 2 * BLOCK), jnp.float32),
        ],
    )
    return pl.pallas_call(
        functools.partial(_mixer_kernel, alpha=alpha),
        grid_spec=grid_spec,
        out_shape=jax.ShapeDtypeStruct(x.shape, jnp.float32),
        compiler_params=pltpu.CompilerParams(
            dimension_semantics=("arbitrary", "arbitrary"), vmem_limit_bytes=VMEM_LIMIT_BYTES),
        name="mixer",
    )(jnp.broadcast_to(sinks[:, None], (N_Q_HEADS, 2 * BLOCK)), x, w_in, w_pool, pool_scale, w_out, ln_g,
      ln_b)


def _ffn_call(h, p, w1, w2, ln_g, ln_b, w_ple, w_gate, b_gate, alpha):
    batch, seq, d = h.shape
    assert seq % FFN_TILE == 0
    return pl.pallas_call(
        functools.partial(_ffn_kernel, alpha=alpha),
        grid=(batch, seq // FFN_TILE),
        in_specs=[
            pl.BlockSpec((None, FFN_TILE, d), lambda b, i: (b, i, 0)),
            pl.BlockSpec((None, FFN_TILE, PLE_DIM), lambda b, i: (b, i, 0)),
            _const_spec((d, D_FF)),
            _const_spec((D_FF, d)),
            _const_spec((1, d)),
            _const_spec((1, d)),
            _const_spec((PLE_DIM, d)),
            _const_spec((d, d)),
            _const_spec((1, d)),
        ],
        out_specs=pl.BlockSpec((None, FFN_TILE, d), lambda b, i: (b, i, 0)),
        out_shape=jax.ShapeDtypeStruct(h.shape, jnp.float32),
        compiler_params=pltpu.CompilerParams(
            dimension_semantics=("arbitrary", "arbitrary"), vmem_limit_bytes=VMEM_LIMIT_BYTES),
        name="ffn",
    )(h, p, w1, w2, ln_g, ln_b, w_ple, w_gate, b_gate)


def kernel(x, p, w_in, w_pool, pool_scale, attn_sinks, w_out, ln1_g, ln1_b, w_ff1, w_ff2, ln2_g, ln2_b,
           w_ple, w_ple_gate, b_ple_gate):
    depth = w_in.shape[0]
    alpha = (2 * depth) ** 0.25
    bf = jnp.bfloat16
    row = lambda v: v.reshape(1, -1)
    h = x
    for i in range(depth):
        h = _mixer_call(h, attn_sinks[i], w_in[i].astype(bf), w_pool[i].astype(bf), row(pool_scale[i]),
                        w_out[i].astype(bf), row(ln1_g[i]), row(ln1_b[i]), alpha)
        h = _ffn_call(h, p[i], w_ff1[i].astype(bf), w_ff2[i].astype(bf), row(ln2_g[i]), row(ln2_b[i]),
                      w_ple[i].astype(bf), w_ple_gate[i].astype(bf), row(b_ple_gate[i]), alpha)
    return h
```

```python
import functools
import math

import jax
import jax.numpy as jnp
from jax import lax
from jax.experimental import pallas as pl
from jax.experimental.pallas import tpu as pltpu

D_MODEL = 1024
PLE_DIM = 256
POOL_WIDTH = 512
POOL_WINDOWS = (2, 4, 8, 16)
POOL_GROUP = 128
HEAD_DIM = 64
ATTN_WIDTH = 512
N_Q_HEADS = 8
N_KV_HEADS = 2
GQA_GROUP = N_Q_HEADS // N_KV_HEADS
KV_WIDTH = N_KV_HEADS * HEAD_DIM
IN_WIDTH = POOL_WIDTH + ATTN_WIDTH + 2 * KV_WIDTH
BLOCK = 128
D_FF = 4 * D_MODEL
LN_EPS = 1e-5
NEG_INF = -1e30

POOL_HALO = 16
MIX_TILE = 1024
MIX_SUB = 256
FFN_TILE = 1024
FF_CHUNK = 512
VMEM_LIMIT_BYTES = 56 * 1024 * 1024

PLAIN_HEADS = tuple(h for h in range(N_Q_HEADS) if h % 2 == h // GQA_GROUP)
SWAP_HEADS = tuple(h for h in range(N_Q_HEADS) if h % 2 != h // GQA_GROUP)


def _layer_norm(y, g, b):
    mu = jnp.mean(y, axis=-1, keepdims=True)
    yc = y - mu
    var = jnp.mean(yc * yc, axis=-1, keepdims=True)
    return yc * lax.rsqrt(var + LN_EPS) * g + b


def _const_spec(shape):
    return pl.BlockSpec(shape, lambda *_: (0,) * len(shape), pipeline_mode=pl.Buffered(1))


def _mixer_kernel(sinks_ref, x_ref, w_in_ref, w_pool_ref, pool_scale_ref, w_out_ref, g_ref, b_ref,
                  wf1_ref, wf2_ref, wple_ref, wgate_ref,
                  o_ref, wf1_bf_ref, wf2_bf_ref, wple_bf_ref, wgate_bf_ref,
                  z_ref, mix_ref, halo_ref, kvprev_ref, logit_ref, *, alpha):
    tile = pl.program_id(1)
    ts = x_ref.shape[0]

    for src, dst in ((wf1_ref, wf1_bf_ref), (wf2_ref, wf2_bf_ref), (wple_ref, wple_bf_ref),
                     (wgate_ref, wgate_bf_ref)):
        dst[...] = src[...].astype(jnp.bfloat16)
    slot = tile % 2

    @pl.when(tile == 0)
    def _():
        halo_ref[0] = jnp.zeros(halo_ref.shape[1:], halo_ref.dtype)
        kvprev_ref[0] = jnp.zeros(kvprev_ref.shape[1:], kvprev_ref.dtype)

    qi = lax.broadcasted_iota(jnp.int32, (BLOCK, 2 * BLOCK), 0)
    kj = lax.broadcasted_iota(jnp.int32, (BLOCK, 2 * BLOCK), 1)
    dist = qi + BLOCK - kj
    in_band = (dist >= 0) & (dist < BLOCK)
    low_half = lax.broadcasted_iota(jnp.int32, (BLOCK, 2 * HEAD_DIM), 1) < HEAD_DIM

    @pl.when((pl.program_id(0) == 0) & (tile == 0))
    def _():
        dist_f = dist.astype(jnp.float32)
        for first in (0, 1):
            ok = in_band & (kj >= BLOCK) if first == 0 else in_band
            for h in range(N_Q_HEADS):
                slope = 2.0 ** (-8.0 * (h + 1) / N_Q_HEADS)
                masked = jnp.where(kj == qi, sinks_ref[h:h + 1, :], NEG_INF)
                logit_ref[first, h] = jnp.where(ok, -slope * dist_f, masked)

    def attn_block(blk, prev):
        rows = slice(blk * BLOCK, (blk + 1) * BLOCK)
        table = jnp.minimum(tile * (ts // BLOCK) + blk, 1)
        valid = in_band & (kj >= BLOCK * (1 - table))

        k_f = z_ref[rows, POOL_WIDTH + ATTN_WIDTH:POOL_WIDTH + ATTN_WIDTH + KV_WIDTH]
        v_f = z_ref[rows, POOL_WIDTH + ATTN_WIDTH + KV_WIDTH:]
        cur = jnp.concatenate(
            [k_f, v_f, pltpu.roll(k_f, HEAD_DIM, axis=1), pltpu.roll(v_f, HEAD_DIM, axis=1)],
            axis=1).astype(jnp.bfloat16)
        band = jnp.concatenate([prev, cur], axis=0)

        outs = {}
        for heads, off in ((PLAIN_HEADS, 0), (SWAP_HEADS, 2 * KV_WIDTH)):
            k_band = band[:, off:off + KV_WIDTH]
            v_band = band[:, off + KV_WIDTH:off + 2 * KV_WIDTH]
            q_parts = []
            for h in heads:
                pair = POOL_WIDTH + (h // 2) * 2 * HEAD_DIM
                q = z_ref[rows, pair:pair + 2 * HEAD_DIM] * (1.0 / math.sqrt(HEAD_DIM))
                keep = low_half if h % 2 == 0 else jnp.logical_not(low_half)
                q_parts.append(jnp.where(keep, q, 0.0).astype(jnp.bfloat16))
            q_stack = jnp.concatenate(q_parts, axis=0)
            scores = lax.dot_general(q_stack, k_band, (((1,), (1,)), ((), ())),
                                     preferred_element_type=jnp.float32)
            p_parts, inv_parts = [], []
            for j, h in enumerate(heads):
                sc = jnp.where(valid, scores[j * BLOCK:(j + 1) * BLOCK], 0.0) + logit_ref[table, h]
                e = jnp.exp(sc - jnp.max(sc, axis=-1, keepdims=True))
                inv_parts.append(1.0 / jnp.sum(e, axis=-1, keepdims=True))
                p_parts.append(jnp.where(valid, e, 0.0).astype(jnp.bfloat16))
            pv = jnp.dot(jnp.concatenate(p_parts, axis=0), v_band,
                         preferred_element_type=jnp.float32)
            for j, h in enumerate(heads):
                outs[h] = pv[j * BLOCK:(j + 1) * BLOCK] * inv_parts[j]
        for pair in range(N_Q_HEADS // 2):
            col = POOL_WIDTH + pair * 2 * HEAD_DIM
            merged = jnp.where(low_half, outs[2 * pair], outs[2 * pair + 1])
            mix_ref[rows, col:col + 2 * HEAD_DIM] = merged.astype(jnp.bfloat16)
        return cur

    halo = halo_ref[slot]
    prev = kvprev_ref[slot]
    for sub in range(ts // MIX_SUB):
        r0 = sub * MIX_SUB
        rows = slice(r0, r0 + MIX_SUB)
        z_ref[rows, :] = jnp.dot(x_ref[rows, :].astype(jnp.bfloat16), w_in_ref[...],
                                 preferred_element_type=jnp.float32)

        pos = tile * ts + r0 + lax.broadcasted_iota(jnp.int32, (MIX_SUB, POOL_GROUP), 0)
        for g, w in enumerate(POOL_WINDOWS):
            cols = slice(g * POOL_GROUP, (g + 1) * POOL_GROUP)
            u = z_ref[rows, cols]
            s = jnp.concatenate([halo[:, cols], u], axis=0)
            shift = 1
            while shift < w:
                s = s + pltpu.roll(s, shift, axis=0)
                shift *= 2
            count = jnp.minimum(pos + 1, w).astype(jnp.float32)
            d = s[POOL_HALO:] / count - u
            y = jnp.dot(d.astype(jnp.bfloat16), w_pool_ref[g], preferred_element_type=jnp.float32)
            mix_ref[rows, cols] = (y * pool_scale_ref[:, cols]).astype(jnp.bfloat16)
        halo = z_ref[r0 + MIX_SUB - POOL_HALO:r0 + MIX_SUB, :POOL_WIDTH]

        for blk in range(r0 // BLOCK, (r0 + MIX_SUB) // BLOCK):
            prev = attn_block(blk, prev)

        mix = jnp.dot(mix_ref[rows, :], w_out_ref[...], preferred_element_type=jnp.float32)
        o_ref[rows, :] = _layer_norm(alpha * x_ref[rows, :] + mix, g_ref[...], b_ref[...])
    halo_ref[1 - slot] = halo
    kvprev_ref[1 - slot] = prev


def _ffn_kernel(h_ref, p_ref, w1_ref, w2_ref, g_ref, b_ref, wple_ref, wgate_ref, bgate_ref, o_ref, *,
                alpha):
    h = h_ref[...]
    hb = h.astype(jnp.bfloat16)
    acc = jnp.zeros(h.shape, jnp.float32)
    for c in range(D_FF // FF_CHUNK):
        cols = slice(c * FF_CHUNK, (c + 1) * FF_CHUNK)
        hid = jnp.maximum(jnp.dot(hb, w1_ref[:, cols], preferred_element_type=jnp.float32), 0.0)
        acc = acc + jnp.dot((hid * hid).astype(jnp.bfloat16), w2_ref[cols, :],
                            preferred_element_type=jnp.float32)
    h2 = _layer_norm(alpha * h + acc, g_ref[...], b_ref[...])
    gate = jax.nn.sigmoid(
        jnp.dot(h2.astype(jnp.bfloat16), wgate_ref[...], preferred_element_type=jnp.float32)
        + bgate_ref[...])
    ple = jnp.dot(p_ref[...].astype(jnp.bfloat16), wple_ref[...], preferred_element_type=jnp.float32)
    o_ref[...] = h2 + gate * ple


def _slab_spec(shape, axis, n_steps, tiles):
    assert shape[axis] % n_steps == 0 and (shape[axis] // n_steps) % (16 if axis == 0 else 128) == 0
    block = tuple(dim // n_steps if a == axis else dim for a, dim in enumerate(shape))
    return pl.BlockSpec(block, lambda b, i: tuple(b * tiles + i if a == axis else 0 for a in range(2)))


def _mixer_call(x, sinks, w_in, w_pool, pool_scale, w_out, ln_g, ln_b, ffn_weights, alpha):
    batch, seq, d = x.shape
    assert d == D_MODEL and seq % MIX_TILE == 0 and MIX_TILE % MIX_SUB == 0 and MIX_SUB % BLOCK == 0
    tiles = seq // MIX_TILE
    slab_specs = [_slab_spec(w.shape, axis, batch * tiles, tiles) for w, axis in zip(ffn_weights, (1, 0, 0, 0))]
    grid_spec = pl.GridSpec(
        grid=(batch, tiles),
        in_specs=[
            _const_spec((N_Q_HEADS, 2 * BLOCK)),
            pl.BlockSpec((None, MIX_TILE, d), lambda b, i: (b, i, 0)),
            _const_spec((d, IN_WIDTH)),
            _const_spec((len(POOL_WINDOWS), POOL_GROUP, POOL_GROUP)),
            _const_spec((1, POOL_WIDTH)),
            _const_spec((POOL_WIDTH + ATTN_WIDTH, d)),
            _const_spec((1, d)),
            _const_spec((1, d)),
            *slab_specs,
        ],
        out_specs=[pl.BlockSpec((None, MIX_TILE, d), lambda b, i: (b, i, 0)), *slab_specs],
        scratch_shapes=[
            pltpu.VMEM((MIX_TILE, IN_WIDTH), jnp.float32),
            pltpu.VMEM((MIX_TILE, POOL_WIDTH + ATTN_WIDTH), jnp.bfloat16),
            pltpu.VMEM((2, POOL_HALO, POOL_WIDTH), jnp.float32),
            pltpu.VMEM((2, BLOCK, 4 * KV_WIDTH), jnp.bfloat16),
            pltpu.VMEM((2, N_Q_HEADS, BLOCK, 2 * BLOCK), jnp.float32),
        ],
    )
    return pl.pallas_call(
        functools.partial(_mixer_kernel, alpha=alpha),
        grid_spec=grid_spec,
        out_shape=[jax.ShapeDtypeStruct(x.shape, jnp.float32),
                   *(jax.ShapeDtypeStruct(w.shape, jnp.bfloat16) for w in ffn_weights)],
        compiler_params=pltpu.CompilerParams(
            dimension_semantics=("arbitrary", "arbitrary"), vmem_limit_bytes=VMEM_LIMIT_BYTES),
        name="mixer",
    )(jnp.broadcast_to(sinks[:, None], (N_Q_HEADS, 2 * BLOCK)), x, w_in, w_pool, pool_scale, w_out, ln_g,
      ln_b, *ffn_weights)


def _ffn_call(h, p, w1, w2, ln_g, ln_b, w_ple, w_gate, b_gate, alpha):
    batch, seq, d = h.shape
    assert seq % FFN_TILE == 0
    return pl.pallas_call(
        functools.partial(_ffn_kernel, alpha=alpha),
        grid=(batch, seq // FFN_TILE),
        in_specs=[
            pl.BlockSpec((None, FFN_TILE, d), lambda b, i: (b, i, 0)),
            pl.BlockSpec((None, FFN_TILE, PLE_DIM), lambda b, i: (b, i, 0)),
            _const_spec((d, D_FF)),
            _const_spec((D_FF, d)),
            _const_spec((1, d)),
            _const_spec((1, d)),
            _const_spec((PLE_DIM, d)),
            _const_spec((d, d)),
            _const_spec((1, d)),
        ],
        out_specs=pl.BlockSpec((None, FFN_TILE, d), lambda b, i: (b, i, 0)),
        out_shape=jax.ShapeDtypeStruct(h.shape, jnp.float32),
        compiler_params=pltpu.CompilerParams(
            dimension_semantics=("arbitrary", "arbitrary"), vmem_limit_bytes=VMEM_LIMIT_BYTES),
        name="ffn",
    )(h, p, w1, w2, ln_g, ln_b, w_ple, w_gate, b_gate)


def kernel(x, p, w_in, w_pool, pool_scale, attn_sinks, w_out, ln1_g, ln1_b, w_ff1, w_ff2, ln2_g, ln2_b,
           w_ple, w_ple_gate, b_ple_gate):
    depth = w_in.shape[0]
    alpha = (2 * depth) ** 0.25
    bf = jnp.bfloat16
    row = lambda v: v.reshape(1, -1)
    h = x
    for i in range(depth):
        h, w1, w2, wple, wgate = _mixer_call(
            h, attn_sinks[i], w_in[i].astype(bf), w_pool[i].astype(bf), row(pool_scale[i]), w_out[i].astype(bf),
            row(ln1_g[i]), row(ln1_b[i]), (w_ff1[i], w_ff2[i], w_ple[i], w_ple_gate[i]), alpha)
        h = _ffn_call(h, p[i], w1, w2, row(ln2_g[i]), row(ln2_b[i]), wple, wgate, row(b_ple_gate[i]), alpha)
    return h
```

```python
import functools
import math

import jax
import jax.numpy as jnp
from jax import lax
from jax.experimental import pallas as pl
from jax.experimental.pallas import tpu as pltpu

D_MODEL = 1024
PLE_DIM = 256
POOL_WIDTH = 512
POOL_WINDOWS = (2, 4, 8, 16)
POOL_GROUP = 128
HEAD_DIM = 64
ATTN_WIDTH = 512
N_Q_HEADS = 8
N_KV_HEADS = 2
GQA_GROUP = N_Q_HEADS // N_KV_HEADS
KV_WIDTH = N_KV_HEADS * HEAD_DIM
IN_WIDTH = POOL_WIDTH + ATTN_WIDTH + 2 * KV_WIDTH
BLOCK = 128
D_FF = 4 * D_MODEL
LN_EPS = 1e-5
NEG_INF = -1e30

POOL_HALO = 16
assert all(w & (w - 1) == 0 and w <= POOL_HALO for w in POOL_WINDOWS)
MIX_TILE = 1024
MIX_SUB = 256
MIX_INFLIGHT = 2
FFN_TILE = 1024
FF_CHUNK = 512
VMEM_LIMIT_BYTES = 56 * 1024 * 1024

PLAIN_HEADS = tuple(h for h in range(N_Q_HEADS) if h % 2 == h // GQA_GROUP)
SWAP_HEADS = tuple(h for h in range(N_Q_HEADS) if h % 2 != h // GQA_GROUP)


def _layer_norm(y, g, b):
    mu = jnp.mean(y, axis=-1, keepdims=True)
    yc = y - mu
    var = jnp.mean(yc * yc, axis=-1, keepdims=True)
    return yc * lax.rsqrt(var + LN_EPS) * g + b


def _zero_after(v):
    bits = lax.bitcast_convert_type(v, jnp.uint32)
    acc = bits[:, 0:128]
    for c in range(1, v.shape[1] // 128):
        acc = acc | bits[:, c * 128:(c + 1) * 128]
    tok = acc[0:8]
    for r in range(1, v.shape[0] // 8):
        tok = tok | acc[r * 8:(r + 1) * 8]
    return (tok >> 16) >> 16


def _const_spec(shape):
    return pl.BlockSpec(shape, lambda *_: (0,) * len(shape), pipeline_mode=pl.Buffered(1))


def _mixer_kernel(sinks_ref, x_ref, w_in_ref, w_pool_ref, pool_scale_ref, w_out_ref, g_ref, b_ref,
                  wf1_ref, wf2_ref, wple_ref, wgate_ref,
                  o_ref, wf1_bf_ref, wf2_bf_ref, wple_bf_ref, wgate_bf_ref,
                  z_ref, mix_ref, halo_ref, kvprev_ref, logit_ref, *, alpha):
    tile = pl.program_id(1)
    ts = x_ref.shape[0]

    for src, dst in ((wf1_ref, wf1_bf_ref), (wf2_ref, wf2_bf_ref), (wple_ref, wple_bf_ref),
                     (wgate_ref, wgate_bf_ref)):
        dst[...] = src[...].astype(jnp.bfloat16)
    slot = tile % 2

    @pl.when(tile == 0)
    def _():
        halo_ref[0] = jnp.zeros(halo_ref.shape[1:], halo_ref.dtype)
        kvprev_ref[0] = jnp.zeros(kvprev_ref.shape[1:], kvprev_ref.dtype)

    qi = lax.broadcasted_iota(jnp.int32, (BLOCK, 2 * BLOCK), 0)
    kj = lax.broadcasted_iota(jnp.int32, (BLOCK, 2 * BLOCK), 1)
    dist = qi + BLOCK - kj
    in_band = (dist >= 0) & (dist < BLOCK)
    low_half = lax.broadcasted_iota(jnp.int32, (BLOCK, 2 * HEAD_DIM), 1) < HEAD_DIM

    @pl.when((pl.program_id(0) == 0) & (tile == 0))
    def _():
        dist_f = dist.astype(jnp.float32)
        for first in (0, 1):
            ok = in_band & (kj >= BLOCK) if first == 0 else in_band
            for h in range(N_Q_HEADS):
                slope = 2.0 ** (-8.0 * (h + 1) / N_Q_HEADS)
                masked = jnp.where(kj == qi, sinks_ref[h:h + 1, :], NEG_INF)
                logit_ref[first, h] = jnp.where(ok, -slope * dist_f, masked)

    def attn_block(blk, prev):
        rows = slice(blk * BLOCK, (blk + 1) * BLOCK)
        table = jnp.minimum(tile * (ts // BLOCK) + blk, 1)
        valid = in_band & (kj >= BLOCK * (1 - table))

        k_f = z_ref[rows, POOL_WIDTH + ATTN_WIDTH:POOL_WIDTH + ATTN_WIDTH + KV_WIDTH]
        v_f = z_ref[rows, POOL_WIDTH + ATTN_WIDTH + KV_WIDTH:]
        cur = jnp.concatenate(
            [k_f, v_f, pltpu.roll(k_f, HEAD_DIM, axis=1), pltpu.roll(v_f, HEAD_DIM, axis=1)],
            axis=1).astype(jnp.bfloat16)
        band = jnp.concatenate([prev, cur], axis=0)

        outs = {}
        for heads, off in ((PLAIN_HEADS, 0), (SWAP_HEADS, 2 * KV_WIDTH)):
            k_band = band[:, off:off + KV_WIDTH]
            v_band = band[:, off + KV_WIDTH:off + 2 * KV_WIDTH]
            q_parts = []
            for h in heads:
                pair = POOL_WIDTH + (h // 2) * 2 * HEAD_DIM
                q = z_ref[rows, pair:pair + 2 * HEAD_DIM] * (1.0 / math.sqrt(HEAD_DIM))
                keep = low_half if h % 2 == 0 else jnp.logical_not(low_half)
                q_parts.append(jnp.where(keep, q, 0.0).astype(jnp.bfloat16))
            q_stack = jnp.concatenate(q_parts, axis=0)
            scores = lax.dot_general(q_stack, k_band, (((1,), (1,)), ((), ())),
                                     preferred_element_type=jnp.float32)
            p_parts, inv_parts = [], []
            for j, h in enumerate(heads):
                sc = jnp.where(valid, scores[j * BLOCK:(j + 1) * BLOCK], 0.0) + logit_ref[table, h]
                e = jnp.exp(sc - jnp.max(sc, axis=-1, keepdims=True))
                inv_parts.append(1.0 / jnp.sum(e, axis=-1, keepdims=True))
                p_parts.append(jnp.where(valid, e, 0.0).astype(jnp.bfloat16))
            pv = jnp.dot(jnp.concatenate(p_parts, axis=0), v_band,
                         preferred_element_type=jnp.float32)
            for j, h in enumerate(heads):
                outs[h] = pv[j * BLOCK:(j + 1) * BLOCK] * inv_parts[j]
        for pair in range(N_Q_HEADS // 2):
            col = POOL_WIDTH + pair * 2 * HEAD_DIM
            merged = jnp.where(low_half, outs[2 * pair], outs[2 * pair + 1])
            mix_ref[rows, col:col + 2 * HEAD_DIM] = merged
        return cur

    halo = halo_ref[slot]
    prev = kvprev_ref[slot]
    finished = []
    for sub in range(ts // MIX_SUB):
        r0 = sub * MIX_SUB
        rows = slice(r0, r0 + MIX_SUB)
        x_sub = x_ref[rows, :]
        if sub >= MIX_INFLIGHT:
            gate = jnp.concatenate([finished[sub - MIX_INFLIGHT]] * (MIX_SUB // 8), axis=0)
            first = lax.bitcast_convert_type(lax.bitcast_convert_type(x_sub[:, 0:128], jnp.uint32) | gate,
                                             jnp.float32)
            x_sub = jnp.concatenate([first, x_sub[:, 128:]], axis=1)
        z_ref[rows, :] = jnp.dot(x_sub.astype(jnp.bfloat16), w_in_ref[...],
                                 preferred_element_type=jnp.float32)

        for g, w in enumerate(POOL_WINDOWS):
            cols = slice(g * POOL_GROUP, (g + 1) * POOL_GROUP)
            u = z_ref[rows, cols]
            s = jnp.concatenate([halo[:, cols], u], axis=0)
            shift = 1
            while shift < w:
                s = s + pltpu.roll(s, shift, axis=0)
                shift *= 2
            mean = s[POOL_HALO:] * (1.0 / w)
            if sub == 0:
                pos = tile * ts + lax.broadcasted_iota(jnp.int32, (POOL_HALO, POOL_GROUP), 0)
                count = jnp.minimum(pos + 1, w).astype(jnp.float32)
                mean = jnp.concatenate([s[POOL_HALO:2 * POOL_HALO] / count, mean[POOL_HALO:]], axis=0)
            d = mean - u
            y = jnp.dot(d.astype(jnp.bfloat16), w_pool_ref[g], preferred_element_type=jnp.float32)
            mix_ref[rows, cols] = y * pool_scale_ref[:, cols]
        halo = z_ref[r0 + MIX_SUB - POOL_HALO:r0 + MIX_SUB, :POOL_WIDTH]

        for blk in range(r0 // BLOCK, (r0 + MIX_SUB) // BLOCK):
            prev = attn_block(blk, prev)

        mix = jnp.dot(mix_ref[rows, :].astype(jnp.bfloat16), w_out_ref[...],
                      preferred_element_type=jnp.float32)
        normed = _layer_norm(alpha * x_ref[rows, :] + mix, g_ref[...], b_ref[...])
        o_ref[rows, :] = normed
        finished.append(_zero_after(normed))
    halo_ref[1 - slot] = halo
    kvprev_ref[1 - slot] = prev


def _ffn_kernel(h_ref, p_ref, w1_ref, w2_ref, g_ref, b_ref, wple_ref, wgate_ref, bgate_ref, o_ref, *,
                alpha):
    h = h_ref[...]
    hb = h.astype(jnp.bfloat16)
    acc = jnp.zeros(h.shape, jnp.float32)
    for c in range(D_FF // FF_CHUNK):
        cols = slice(c * FF_CHUNK, (c + 1) * FF_CHUNK)
        hid = jnp.maximum(jnp.dot(hb, w1_ref[:, cols], preferred_element_type=jnp.float32), 0.0)
        acc = acc + jnp.dot((hid * hid).astype(jnp.bfloat16), w2_ref[cols, :],
                            preferred_element_type=jnp.float32)
    h2 = _layer_norm(alpha * h + acc, g_ref[...], b_ref[...])
    gate = jax.nn.sigmoid(
        jnp.dot(h2.astype(jnp.bfloat16), wgate_ref[...], preferred_element_type=jnp.float32)
        + bgate_ref[...])
    ple = jnp.dot(p_ref[...].astype(jnp.bfloat16), wple_ref[...], preferred_element_type=jnp.float32)
    o_ref[...] = h2 + gate * ple


def _slab_spec(shape, axis, n_steps, tiles):
    assert shape[axis] % n_steps == 0 and (shape[axis] // n_steps) % (16 if axis == 0 else 128) == 0
    block = tuple(dim // n_steps if a == axis else dim for a, dim in enumerate(shape))
    return pl.BlockSpec(block, lambda b, i: tuple(b * tiles + i if a == axis else 0 for a in range(2)))


def _mixer_call(x, sinks, w_in, w_pool, pool_scale, w_out, ln_g, ln_b, ffn_weights, alpha):
    batch, seq, d = x.shape
    assert d == D_MODEL and seq % MIX_TILE == 0 and MIX_TILE % MIX_SUB == 0 and MIX_SUB % BLOCK == 0
    tiles = seq // MIX_TILE
    slab_specs = [_slab_spec(w.shape, axis, batch * tiles, tiles) for w, axis in zip(ffn_weights, (1, 0, 0, 0))]
    grid_spec = pl.GridSpec(
        grid=(batch, tiles),
        in_specs=[
            _const_spec((N_Q_HEADS, 2 * BLOCK)),
            pl.BlockSpec((None, MIX_TILE, d), lambda b, i: (b, i, 0)),
            _const_spec((d, IN_WIDTH)),
            _const_spec((len(POOL_WINDOWS), POOL_GROUP, POOL_GROUP)),
            _const_spec((1, POOL_WIDTH)),
            _const_spec((POOL_WIDTH + ATTN_WIDTH, d)),
            _const_spec((1, d)),
            _const_spec((1, d)),
            *slab_specs,
        ],
        out_specs=[pl.BlockSpec((None, MIX_TILE, d), lambda b, i: (b, i, 0)), *slab_specs],
        scratch_shapes=[
            pltpu.VMEM((MIX_TILE, IN_WIDTH), jnp.float32),
            pltpu.VMEM((MIX_TILE, POOL_WIDTH + ATTN_WIDTH), jnp.float32),
            pltpu.VMEM((2, POOL_HALO, POOL_WIDTH), jnp.float32),
            pltpu.VMEM((2, BLOCK, 4 * KV_WIDTH), jnp.bfloat16),
            pltpu.VMEM((2, N_Q_HEADS, BLOCK, 2 * BLOCK), jnp.float32),
        ],
    )
    return pl.pallas_call(
        functools.partial(_mixer_kernel, alpha=alpha),
        grid_spec=grid_spec,
        out_shape=[jax.ShapeDtypeStruct(x.shape, jnp.float32),
                   *(jax.ShapeDtypeStruct(w.shape, jnp.bfloat16) for w in ffn_weights)],
        compiler_params=pltpu.CompilerParams(
            dimension_semantics=("arbitrary", "arbitrary"), vmem_limit_bytes=VMEM_LIMIT_BYTES),
        name="mixer",
    )(jnp.broadcast_to(sinks[:, None], (N_Q_HEADS, 2 * BLOCK)), x, w_in, w_pool, pool_scale, w_out, ln_g,
      ln_b, *ffn_weights)


def _ffn_call(h, p, w1, w2, ln_g, ln_b, w_ple, w_gate, b_gate, alpha):
    batch, seq, d = h.shape
    assert seq % FFN_TILE == 0
    return pl.pallas_call(
        functools.partial(_ffn_kernel, alpha=alpha),
        grid=(batch, seq // FFN_TILE),
        in_specs=[
            pl.BlockSpec((None, FFN_TILE, d), lambda b, i: (b, i, 0)),
            pl.BlockSpec((None, FFN_TILE, PLE_DIM), lambda b, i: (b, i, 0)),
            _const_spec((d, D_FF)),
            _const_spec((D_FF, d)),
            _const_spec((1, d)),
            _const_spec((1, d)),
            _const_spec((PLE_DIM, d)),
            _const_spec((d, d)),
            _const_spec((1, d)),
        ],
        out_specs=pl.BlockSpec((None, FFN_TILE, d), lambda b, i: (b, i, 0)),
        out_shape=jax.ShapeDtypeStruct(h.shape, jnp.float32),
        compiler_params=pltpu.CompilerParams(
            dimension_semantics=("arbitrary", "arbitrary"), vmem_limit_bytes=VMEM_LIMIT_BYTES),
        name="ffn",
    )(h, p, w1, w2, ln_g, ln_b, w_ple, w_gate, b_gate)


def kernel(x, p, w_in, w_pool, pool_scale, attn_sinks, w_out, ln1_g, ln1_b, w_ff1, w_ff2, ln2_g, ln2_b,
           w_ple, w_ple_gate, b_ple_gate):
    depth = w_in.shape[0]
    alpha = (2 * depth) ** 0.25
    bf = jnp.bfloat16
    row = lambda v: v.reshape(1, -1)
    h = x
    for i in range(depth):
        h, w1, w2, wple, wgate = _mixer_call(
            h, attn_sinks[i], w_in[i].astype(bf), w_pool[i].astype(bf), row(pool_scale[i]), w_out[i].astype(bf),
            row(ln1_g[i]), row(ln1_b[i]), (w_ff1[i], w_ff2[i], w_ple[i], w_ple_gate[i]), alpha)
        h = _ffn_call(h, p[i], w1, w2, row(ln2_g[i]), row(ln2_b[i]), wple, wgate, row(b_ple_gate[i]), alpha)
    return h
```

```python
import functools
import math

import jax
import jax.numpy as jnp
from jax import lax
from jax.experimental import pallas as pl
from jax.experimental.pallas import tpu as pltpu

D_MODEL = 1024
PLE_DIM = 256
POOL_WIDTH = 512
POOL_WINDOWS = (2, 4, 8, 16)
POOL_GROUP = 128
HEAD_DIM = 64
ATTN_WIDTH = 512
N_Q_HEADS = 8
N_KV_HEADS = 2
GQA_GROUP = N_Q_HEADS // N_KV_HEADS
KV_WIDTH = N_KV_HEADS * HEAD_DIM
IN_WIDTH = POOL_WIDTH + ATTN_WIDTH + 2 * KV_WIDTH
BLOCK = 128
D_FF = 4 * D_MODEL
LN_EPS = 1e-5
NEG_INF = -1e30

POOL_HALO = 16
assert all(w & (w - 1) == 0 and w <= POOL_HALO for w in POOL_WINDOWS)
MIX_TILE = 1024
MIX_SUB = 1024
MIX_INFLIGHT = 2
FFN_TILE = 1024
FF_CHUNK = 512
VMEM_LIMIT_BYTES = 56 * 1024 * 1024

PLAIN_HEADS = tuple(h for h in range(N_Q_HEADS) if h % 2 == h // GQA_GROUP)
SWAP_HEADS = tuple(h for h in range(N_Q_HEADS) if h % 2 != h // GQA_GROUP)


def _layer_norm(y, g, b):
    mu = jnp.mean(y, axis=-1, keepdims=True)
    yc = y - mu
    var = jnp.mean(yc * yc, axis=-1, keepdims=True)
    return yc * lax.rsqrt(var + LN_EPS) * g + b


def _zero_after(v):
    bits = lax.bitcast_convert_type(v, jnp.uint32)
    acc = bits[:, 0:128]
    for c in range(1, v.shape[1] // 128):
        acc = acc | bits[:, c * 128:(c + 1) * 128]
    tok = acc[0:8]
    for r in range(1, v.shape[0] // 8):
        tok = tok | acc[r * 8:(r + 1) * 8]
    return (tok >> 16) >> 16


def _const_spec(shape):
    return pl.BlockSpec(shape, lambda *_: (0,) * len(shape), pipeline_mode=pl.Buffered(1))


def _mixer_kernel(sinks_ref, x_ref, w_in_ref, w_pool_ref, pool_scale_ref, w_out_ref, g_ref, b_ref,
                  wf1_ref, wf2_ref, wple_ref, wgate_ref,
                  o_ref, wf1_bf_ref, wf2_bf_ref, wple_bf_ref, wgate_bf_ref,
                  z_ref, mix_ref, halo_ref, kvprev_ref, logit_ref, *, alpha):
    tile = pl.program_id(1)
    ts = x_ref.shape[0]

    for src, dst in ((wf1_ref, wf1_bf_ref), (wf2_ref, wf2_bf_ref), (wple_ref, wple_bf_ref),
                     (wgate_ref, wgate_bf_ref)):
        dst[...] = src[...].astype(jnp.bfloat16)
    slot = tile % 2

    @pl.when(tile == 0)
    def _():
        halo_ref[0] = jnp.zeros(halo_ref.shape[1:], halo_ref.dtype)
        kvprev_ref[0] = jnp.zeros(kvprev_ref.shape[1:], kvprev_ref.dtype)

    qi = lax.broadcasted_iota(jnp.int32, (BLOCK, 2 * BLOCK), 0)
    kj = lax.broadcasted_iota(jnp.int32, (BLOCK, 2 * BLOCK), 1)
    dist = qi + BLOCK - kj
    in_band = (dist >= 0) & (dist < BLOCK)
    low_half = lax.broadcasted_iota(jnp.int32, (BLOCK, 2 * HEAD_DIM), 1) < HEAD_DIM

    @pl.when((pl.program_id(0) == 0) & (tile == 0))
    def _():
        dist_f = dist.astype(jnp.float32)
        for first in (0, 1):
            ok = in_band & (kj >= BLOCK) if first == 0 else in_band
            for h in range(N_Q_HEADS):
                slope = 2.0 ** (-8.0 * (h + 1) / N_Q_HEADS)
                masked = jnp.where(kj == qi, sinks_ref[h:h + 1, :], NEG_INF)
                logit_ref[first, h] = jnp.where(ok, -slope * dist_f, masked)

    def attn_block(blk, prev):
        rows = slice(blk * BLOCK, (blk + 1) * BLOCK)
        table = jnp.minimum(tile * (ts // BLOCK) + blk, 1)
        valid = in_band & (kj >= BLOCK * (1 - table))

        k_f = z_ref[rows, POOL_WIDTH + ATTN_WIDTH:POOL_WIDTH + ATTN_WIDTH + KV_WIDTH]
        v_f = z_ref[rows, POOL_WIDTH + ATTN_WIDTH + KV_WIDTH:]
        cur = jnp.concatenate(
            [k_f, v_f, pltpu.roll(k_f, HEAD_DIM, axis=1), pltpu.roll(v_f, HEAD_DIM, axis=1)],
            axis=1).astype(jnp.bfloat16)
        band = jnp.concatenate([prev, cur], axis=0)

        outs = {}
        for heads, off in ((PLAIN_HEADS, 0), (SWAP_HEADS, 2 * KV_WIDTH)):
            k_band = band[:, off:off + KV_WIDTH]
            v_band = band[:, off + KV_WIDTH:off + 2 * KV_WIDTH]
            q_parts = []
            for h in heads:
                pair = POOL_WIDTH + (h // 2) * 2 * HEAD_DIM
                q = z_ref[rows, pair:pair + 2 * HEAD_DIM] * (1.0 / math.sqrt(HEAD_DIM))
                keep = low_half if h % 2 == 0 else jnp.logical_not(low_half)
                q_parts.append(jnp.where(keep, q, 0.0).astype(jnp.bfloat16))
            q_stack = jnp.concatenate(q_parts, axis=0)
            scores = lax.dot_general(q_stack, k_band, (((1,), (1,)), ((), ())),
                                     preferred_element_type=jnp.float32)
            p_parts, inv_parts = [], []
            for j, h in enumerate(heads):
                sc = jnp.where(valid, scores[j * BLOCK:(j + 1) * BLOCK], 0.0) + logit_ref[table, h]
                e = jnp.exp(sc - jnp.max(sc, axis=-1, keepdims=True))
                inv_parts.append(1.0 / jnp.sum(e, axis=-1, keepdims=True))
                p_parts.append(jnp.where(valid, e, 0.0).astype(jnp.bfloat16))
            pv = jnp.dot(jnp.concatenate(p_parts, axis=0), v_band,
                         preferred_element_type=jnp.float32)
            for j, h in enumerate(heads):
                outs[h] = pv[j * BLOCK:(j + 1) * BLOCK] * inv_parts[j]
        for pair in range(N_Q_HEADS // 2):
            col = POOL_WIDTH + pair * 2 * HEAD_DIM
            merged = jnp.where(low_half, outs[2 * pair], outs[2 * pair + 1])
            mix_ref[rows, col:col + 2 * HEAD_DIM] = merged
        return cur

    halo = halo_ref[slot]
    prev = kvprev_ref[slot]
    finished = []
    for sub in range(ts // MIX_SUB):
        r0 = sub * MIX_SUB
        rows = slice(r0, r0 + MIX_SUB)
        x_sub = x_ref[rows, :]
        if sub >= MIX_INFLIGHT:
            gate = jnp.concatenate([finished[sub - MIX_INFLIGHT]] * (MIX_SUB // 8), axis=0)
            first = lax.bitcast_convert_type(lax.bitcast_convert_type(x_sub[:, 0:128], jnp.uint32) | gate,
                                             jnp.float32)
            x_sub = jnp.concatenate([first, x_sub[:, 128:]], axis=1)
        z_ref[rows, :] = jnp.dot(x_sub.astype(jnp.bfloat16), w_in_ref[...],
                                 preferred_element_type=jnp.float32)

        for g, w in enumerate(POOL_WINDOWS):
            cols = slice(g * POOL_GROUP, (g + 1) * POOL_GROUP)
            u = z_ref[rows, cols]
            s = jnp.concatenate([halo[:, cols], u], axis=0)
            shift = 1
            while shift < w:
                s = s + pltpu.roll(s, shift, axis=0)
                shift *= 2
            mean = s[POOL_HALO:] * (1.0 / w)
            if sub == 0:
                pos = tile * ts + lax.broadcasted_iota(jnp.int32, (POOL_HALO, POOL_GROUP), 0)
                count = jnp.minimum(pos + 1, w).astype(jnp.float32)
                mean = jnp.concatenate([s[POOL_HALO:2 * POOL_HALO] / count, mean[POOL_HALO:]], axis=0)
            d = mean - u
            y = jnp.dot(d.astype(jnp.bfloat16), w_pool_ref[g], preferred_element_type=jnp.float32)
            mix_ref[rows, cols] = y * pool_scale_ref[:, cols]
        halo = z_ref[r0 + MIX_SUB - POOL_HALO:r0 + MIX_SUB, :POOL_WIDTH]

        for blk in range(r0 // BLOCK, (r0 + MIX_SUB) // BLOCK):
            prev = attn_block(blk, prev)

        mix = jnp.dot(mix_ref[rows, :].astype(jnp.bfloat16), w_out_ref[...],
                      preferred_element_type=jnp.float32)
        normed = _layer_norm(alpha * x_ref[rows, :] + mix, g_ref[...], b_ref[...])
        o_ref[rows, :] = normed
        finished.append(_zero_after(normed))
    halo_ref[1 - slot] = halo
    kvprev_ref[1 - slot] = prev


def _ffn_kernel(h_ref, p_ref, w1_ref, w2_ref, g_ref, b_ref, wple_ref, wgate_ref, bgate_ref, o_ref, *,
                alpha):
    h = h_ref[...]
    hb = h.astype(jnp.bfloat16)
    acc = jnp.zeros(h.shape, jnp.float32)
    for c in range(D_FF // FF_CHUNK):
        cols = slice(c * FF_CHUNK, (c + 1) * FF_CHUNK)
        hid = jnp.maximum(jnp.dot(hb, w1_ref[:, cols], preferred_element_type=jnp.float32), 0.0)
        acc = acc + jnp.dot((hid * hid).astype(jnp.bfloat16), w2_ref[cols, :],
                            preferred_element_type=jnp.float32)
    h2 = _layer_norm(alpha * h + acc, g_ref[...], b_ref[...])
    gate = jax.nn.sigmoid(
        jnp.dot(h2.astype(jnp.bfloat16), wgate_ref[...], preferred_element_type=jnp.float32)
        + bgate_ref[...])
    ple = jnp.dot(p_ref[...].astype(jnp.bfloat16), wple_ref[...], preferred_element_type=jnp.float32)
    o_ref[...] = h2 + gate * ple


def _slab_spec(shape, axis, n_steps, tiles):
    assert shape[axis] % n_steps == 0 and (shape[axis] // n_steps) % (16 if axis == 0 else 128) == 0
    block = tuple(dim // n_steps if a == axis else dim for a, dim in enumerate(shape))
    return pl.BlockSpec(block, lambda b, i: tuple(b * tiles + i if a == axis else 0 for a in range(2)))


def _mixer_call(x, sinks, w_in, w_pool, pool_scale, w_out, ln_g, ln_b, ffn_weights, alpha):
    batch, seq, d = x.shape
    assert d == D_MODEL and seq % MIX_TILE == 0 and MIX_TILE % MIX_SUB == 0 and MIX_SUB % BLOCK == 0
    tiles = seq // MIX_TILE
    slab_specs = [_slab_spec(w.shape, axis, batch * tiles, tiles) for w, axis in zip(ffn_weights, (1, 0, 0, 0))]
    grid_spec = pl.GridSpec(
        grid=(batch, tiles),
        in_specs=[
            _const_spec((N_Q_HEADS, 2 * BLOCK)),
            pl.BlockSpec((None, MIX_TILE, d), lambda b, i: (b, i, 0)),
            _const_spec((d, IN_WIDTH)),
            _const_spec((len(POOL_WINDOWS), POOL_GROUP, POOL_GROUP)),
            _const_spec((1, POOL_WIDTH)),
            _const_spec((POOL_WIDTH + ATTN_WIDTH, d)),
            _const_spec((1, d)),
            _const_spec((1, d)),
            *slab_specs,
        ],
        out_specs=[pl.BlockSpec((None, MIX_TILE, d), lambda b, i: (b, i, 0)), *slab_specs],
        scratch_shapes=[
            pltpu.VMEM((MIX_TILE, IN_WIDTH), jnp.float32),
            pltpu.VMEM((MIX_TILE, POOL_WIDTH + ATTN_WIDTH), jnp.float32),
            pltpu.VMEM((2, POOL_HALO, POOL_WIDTH), jnp.float32),
            pltpu.VMEM((2, BLOCK, 4 * KV_WIDTH), jnp.bfloat16),
            pltpu.VMEM((2, N_Q_HEADS, BLOCK, 2 * BLOCK), jnp.float32),
        ],
    )
    return pl.pallas_call(
        functools.partial(_mixer_kernel, alpha=alpha),
        grid_spec=grid_spec,
        out_shape=[jax.ShapeDtypeStruct(x.shape, jnp.float32),
                   *(jax.ShapeDtypeStruct(w.shape, jnp.bfloat16) for w in ffn_weights)],
        compiler_params=pltpu.CompilerParams(
            dimension_semantics=("arbitrary", "arbitrary"), vmem_limit_bytes=VMEM_LIMIT_BYTES),
        name="mixer",
    )(jnp.broadcast_to(sinks[:, None], (N_Q_HEADS, 2 * BLOCK)), x, w_in, w_pool, pool_scale, w_out, ln_g,
      ln_b, *ffn_weights)


def _ffn_call(h, p, w1, w2, ln_g, ln_b, w_ple, w_gate, b_gate, alpha):
    batch, seq, d = h.shape
    assert seq % FFN_TILE == 0
    return pl.pallas_call(
        functools.partial(_ffn_kernel, alpha=alpha),
        grid=(batch, seq // FFN_TILE),
        in_specs=[
            pl.BlockSpec((None, FFN_TILE, d), lambda b, i: (b, i, 0)),
            pl.BlockSpec((None, FFN_TILE, PLE_DIM), lambda b, i: (b, i, 0)),
            _const_spec((d, D_FF)),
            _const_spec((D_FF, d)),
            _const_spec((1, d)),
            _const_spec((1, d)),
            _const_spec((PLE_DIM, d)),
            _const_spec((d, d)),
            _const_spec((1, d)),
        ],
        out_specs=pl.BlockSpec((None, FFN_TILE, d), lambda b, i: (b, i, 0)),
        out_shape=jax.ShapeDtypeStruct(h.shape, jnp.float32),
        compiler_params=pltpu.CompilerParams(
            dimension_semantics=("arbitrary", "arbitrary"), vmem_limit_bytes=VMEM_LIMIT_BYTES),
        name="ffn",
    )(h, p, w1, w2, ln_g, ln_b, w_ple, w_gate, b_gate)


def kernel(x, p, w_in, w_pool, pool_scale, attn_sinks, w_out, ln1_g, ln1_b, w_ff1, w_ff2, ln2_g, ln2_b,
           w_ple, w_ple_gate, b_ple_gate):
    depth = w_in.shape[0]
    alpha = (2 * depth) ** 0.25
    bf = jnp.bfloat16
    row = lambda v: v.reshape(1, -1)
    h = x
    for i in range(depth):
        h, w1, w2, wple, wgate = _mixer_call(
            h, attn_sinks[i], w_in[i].astype(bf), w_pool[i].astype(bf), row(pool_scale[i]), w_out[i].astype(bf),
            row(ln1_g[i]), row(ln1_b[i]), (w_ff1[i], w_ff2[i], w_ple[i], w_ple_gate[i]), alpha)
        h = _ffn_call(h, p[i], w1, w2, row(ln2_g[i]), row(ln2_b[i]), wple, wgate, row(b_ple_gate[i]), alpha)
    return h
```

```python
import functools
import math

import jax
import jax.numpy as jnp
from jax import lax
from jax.experimental import pallas as pl
from jax.experimental.pallas import tpu as pltpu

D_MODEL = 1024
PLE_DIM = 256
POOL_WIDTH = 512
POOL_WINDOWS = (2, 4, 8, 16)
POOL_GROUP = 128
HEAD_DIM = 64
ATTN_WIDTH = 512
N_Q_HEADS = 8
N_KV_HEADS = 2
GQA_GROUP = N_Q_HEADS // N_KV_HEADS
KV_WIDTH = N_KV_HEADS * HEAD_DIM
IN_WIDTH = POOL_WIDTH + ATTN_WIDTH + 2 * KV_WIDTH
BLOCK = 128
D_FF = 4 * D_MODEL
LN_EPS = 1e-5
NEG_INF = -1e30

POOL_HALO = 16
assert all(w & (w - 1) == 0 and w <= POOL_HALO for w in POOL_WINDOWS)
MIX_TILE = 1024
FFN_TILE = 1024
FF_CHUNK = 512
VMEM_LIMIT_BYTES = 56 * 1024 * 1024

assert N_KV_HEADS == 2 and 2 * HEAD_DIM == 128
PAIR_HEADS = tuple((j, GQA_GROUP + j) for j in range(GQA_GROUP))
HEAD_ORDER = tuple(h for pair in PAIR_HEADS for h in pair)


def _layer_norm(y, g, b):
    mu = jnp.mean(y, axis=-1, keepdims=True)
    yc = y - mu
    var = jnp.mean(yc * yc, axis=-1, keepdims=True)
    return yc * lax.rsqrt(var + LN_EPS) * g + b


def _const_spec(shape):
    return pl.BlockSpec(shape, lambda *_: (0,) * len(shape), pipeline_mode=pl.Buffered(1))


def _mixer_kernel(sinks_ref, x_ref, w_in_ref, w_pool_ref, pool_scale_ref, w_out_ref, g_ref, b_ref,
                  wf1_ref, wf2_ref, wple_ref, wgate_ref,
                  o_ref, wf1_bf_ref, wf2_bf_ref, wple_bf_ref, wgate_bf_ref,
                  z_ref, mix_ref, halo_ref, kvprev_ref, logit_ref, *, alpha):
    tile = pl.program_id(1)
    ts = x_ref.shape[0]

    for src, dst in ((wf1_ref, wf1_bf_ref), (wf2_ref, wf2_bf_ref), (wple_ref, wple_bf_ref),
                     (wgate_ref, wgate_bf_ref)):
        dst[...] = src[...].astype(jnp.bfloat16)

    slot = tile % 2

    @pl.when(tile == 0)
    def _():
        halo_ref[0] = jnp.zeros(halo_ref.shape[1:], halo_ref.dtype)
        kvprev_ref[0] = jnp.zeros(kvprev_ref.shape[1:], kvprev_ref.dtype)

    qi = lax.broadcasted_iota(jnp.int32, (BLOCK, 2 * BLOCK), 0)
    kj = lax.broadcasted_iota(jnp.int32, (BLOCK, 2 * BLOCK), 1)
    dist = qi + BLOCK - kj
    in_band = (dist >= 0) & (dist < BLOCK)
    low_half = lax.broadcasted_iota(jnp.int32, (BLOCK, 2 * HEAD_DIM), 1) < HEAD_DIM

    @pl.when((pl.program_id(0) == 0) & (tile == 0))
    def _():
        dist_f = dist.astype(jnp.float32)
        for first in (0, 1):
            ok = in_band & (kj >= BLOCK) if first == 0 else in_band
            for h in range(N_Q_HEADS):
                slope = 2.0 ** (-8.0 * (h + 1) / N_Q_HEADS)
                masked = jnp.where(kj == qi, sinks_ref[h:h + 1, :], NEG_INF)
                logit_ref[first, h] = jnp.where(ok, -slope * dist_f, masked)

    z_ref[...] = jnp.dot(x_ref[...].astype(jnp.bfloat16), w_in_ref[...], preferred_element_type=jnp.float32)

    halo = halo_ref[slot]
    for g, w in enumerate(POOL_WINDOWS):
        cols = slice(g * POOL_GROUP, (g + 1) * POOL_GROUP)
        u = z_ref[:, cols]
        s = jnp.concatenate([halo[:, cols], u], axis=0)
        shift = 1
        while shift < w:
            s = s + pltpu.roll(s, shift, axis=0)
            shift *= 2
        pos = tile * ts + lax.broadcasted_iota(jnp.int32, (POOL_HALO, POOL_GROUP), 0)
        count = jnp.minimum(pos + 1, w).astype(jnp.float32)
        mean = jnp.concatenate(
            [s[POOL_HALO:2 * POOL_HALO] / count, s[2 * POOL_HALO:] * (1.0 / w)], axis=0)
        y = jnp.dot((mean - u).astype(jnp.bfloat16), w_pool_ref[g], preferred_element_type=jnp.float32)
        mix_ref[:, cols] = y * pool_scale_ref[:, cols]
    halo_ref[1 - slot] = z_ref[ts - POOL_HALO:, :POOL_WIDTH]

    def attn_block(blk, prev):
        rows = slice(blk * BLOCK, (blk + 1) * BLOCK)
        table = jnp.minimum(tile * (ts // BLOCK) + blk, 1)
        valid = in_band & (kj >= BLOCK * (1 - table))

        cur = z_ref[rows, POOL_WIDTH + ATTN_WIDTH:].astype(jnp.bfloat16)
        band = jnp.concatenate([prev, cur], axis=0)
        q_parts = []
        for j, pair in enumerate(PAIR_HEADS):
            col = POOL_WIDTH + j * 2 * HEAD_DIM
            q = z_ref[rows, col:col + 2 * HEAD_DIM] * (1.0 / math.sqrt(HEAD_DIM))
            q_parts.append(jnp.where(low_half, q, 0.0).astype(jnp.bfloat16))
            q_parts.append(jnp.where(low_half, 0.0, q).astype(jnp.bfloat16))
        scores = lax.dot_general(jnp.concatenate(q_parts, axis=0), band[:, :KV_WIDTH],
                                 (((1,), (1,)), ((), ())),
                                 preferred_element_type=jnp.float32)
        p_parts, inv_parts = [], []
        for i, h in enumerate(HEAD_ORDER):
            sc = jnp.where(valid, scores[i * BLOCK:(i + 1) * BLOCK], 0.0) + logit_ref[table, h]
            e = jnp.exp(sc - jnp.max(sc, axis=-1, keepdims=True))
            inv_parts.append(1.0 / jnp.sum(e, axis=-1, keepdims=True))
            p_parts.append(jnp.where(valid, e, 0.0).astype(jnp.bfloat16))
        pv = jnp.dot(jnp.concatenate(p_parts, axis=0), band[:, KV_WIDTH:],
                     preferred_element_type=jnp.float32)
        for j in range(len(PAIR_HEADS)):
            lo = pv[2 * j * BLOCK:(2 * j + 1) * BLOCK] * inv_parts[2 * j]
            hi = pv[(2 * j + 1) * BLOCK:(2 * j + 2) * BLOCK] * inv_parts[2 * j + 1]
            col = POOL_WIDTH + j * 2 * HEAD_DIM
            mix_ref[rows, col:col + 2 * HEAD_DIM] = jnp.where(low_half, lo, hi)
        return cur

    prev = kvprev_ref[slot]
    for blk in range(ts // BLOCK):
        prev = attn_block(blk, prev)
    kvprev_ref[1 - slot] = prev

    mix = jnp.dot(mix_ref[...].astype(jnp.bfloat16), w_out_ref[...], preferred_element_type=jnp.float32)
    o_ref[...] = _layer_norm(alpha * x_ref[...] + mix, g_ref[...], b_ref[...])


def _ffn_kernel(h_ref, p_ref, w1_ref, w2_ref, g_ref, b_ref, wple_ref, wgate_ref, bgate_ref, o_ref, *,
                alpha):
    h = h_ref[...]
    hb = h.astype(jnp.bfloat16)
    acc = jnp.zeros(h.shape, jnp.float32)
    for c in range(D_FF // FF_CHUNK):
        cols = slice(c * FF_CHUNK, (c + 1) * FF_CHUNK)
        hid = jnp.maximum(jnp.dot(hb, w1_ref[:, cols], preferred_element_type=jnp.float32), 0.0)
        acc = acc + jnp.dot((hid * hid).astype(jnp.bfloat16), w2_ref[cols, :],
                            preferred_element_type=jnp.float32)
    h2 = _layer_norm(alpha * h + acc, g_ref[...], b_ref[...])
    gate = jax.nn.sigmoid(
        jnp.dot(h2.astype(jnp.bfloat16), wgate_ref[...], preferred_element_type=jnp.float32)
        + bgate_ref[...])
    ple = jnp.dot(p_ref[...].astype(jnp.bfloat16), wple_ref[...], preferred_element_type=jnp.float32)
    o_ref[...] = h2 + gate * ple


def _slab_spec(shape, axis, n_steps, tiles):
    assert shape[axis] % n_steps == 0 and (shape[axis] // n_steps) % (16 if axis == 0 else 128) == 0
    block = tuple(dim // n_steps if a == axis else dim for a, dim in enumerate(shape))
    return pl.BlockSpec(block, lambda b, i: tuple(b * tiles + i if a == axis else 0 for a in range(2)))


def _mixer_call(x, sinks, w_in, w_pool, pool_scale, w_out, ln_g, ln_b, ffn_weights, alpha):
    batch, seq, d = x.shape
    assert d == D_MODEL and seq % MIX_TILE == 0 and MIX_TILE % BLOCK == 0
    tiles = seq // MIX_TILE
    slab_specs = [_slab_spec(w.shape, axis, batch * tiles, tiles) for w, axis in zip(ffn_weights, (1, 0, 0, 0))]
    grid_spec = pl.GridSpec(
        grid=(batch, tiles),
        in_specs=[
            _const_spec((N_Q_HEADS, 2 * BLOCK)),
            pl.BlockSpec((None, MIX_TILE, d), lambda b, i: (b, i, 0)),
            _const_spec((d, IN_WIDTH)),
            _const_spec((len(POOL_WINDOWS), POOL_GROUP, POOL_GROUP)),
            _const_spec((1, POOL_WIDTH)),
            _const_spec((POOL_WIDTH + ATTN_WIDTH, d)),
            _const_spec((1, d)),
            _const_spec((1, d)),
            *slab_specs,
        ],
        out_specs=[pl.BlockSpec((None, MIX_TILE, d), lambda b, i: (b, i, 0)), *slab_specs],
        scratch_shapes=[
            pltpu.VMEM((MIX_TILE, IN_WIDTH), jnp.float32),
            pltpu.VMEM((MIX_TILE, POOL_WIDTH + ATTN_WIDTH), jnp.float32),
            pltpu.VMEM((2, POOL_HALO, POOL_WIDTH), jnp.float32),
            pltpu.VMEM((2, BLOCK, 2 * KV_WIDTH), jnp.bfloat16),
            pltpu.VMEM((2, N_Q_HEADS, BLOCK, 2 * BLOCK), jnp.float32),
        ],
    )
    return pl.pallas_call(
        functools.partial(_mixer_kernel, alpha=alpha),
        grid_spec=grid_spec,
        out_shape=[jax.ShapeDtypeStruct(x.shape, jnp.float32),
                   *(jax.ShapeDtypeStruct(w.shape, jnp.bfloat16) for w in ffn_weights)],
        compiler_params=pltpu.CompilerParams(
            dimension_semantics=("arbitrary", "arbitrary"), vmem_limit_bytes=VMEM_LIMIT_BYTES),
        name="mixer",
    )(jnp.broadcast_to(sinks[:, None], (N_Q_HEADS, 2 * BLOCK)), x, w_in, w_pool, pool_scale, w_out, ln_g,
      ln_b, *ffn_weights)


def _ffn_call(h, p, w1, w2, ln_g, ln_b, w_ple, w_gate, b_gate, alpha):
    batch, seq, d = h.shape
    assert seq % FFN_TILE == 0
    return pl.pallas_call(
        functools.partial(_ffn_kernel, alpha=alpha),
        grid=(batch, seq // FFN_TILE),
        in_specs=[
            pl.BlockSpec((None, FFN_TILE, d), lambda b, i: (b, i, 0)),
            pl.BlockSpec((None, FFN_TILE, PLE_DIM), lambda b, i: (b, i, 0)),
            _const_spec((d, D_FF)),
            _const_spec((D_FF, d)),
            _const_spec((1, d)),
            _const_spec((1, d)),
            _const_spec((PLE_DIM, d)),
            _const_spec((d, d)),
            _const_spec((1, d)),
        ],
        out_specs=pl.BlockSpec((None, FFN_TILE, d), lambda b, i: (b, i, 0)),
        out_shape=jax.ShapeDtypeStruct(h.shape, jnp.float32),
        compiler_params=pltpu.CompilerParams(
            dimension_semantics=("arbitrary", "arbitrary"), vmem_limit_bytes=VMEM_LIMIT_BYTES),
        name="ffn",
    )(h, p, w1, w2, ln_g, ln_b, w_ple, w_gate, b_gate)


def _pair_heads(w, axis):
    shape = w.shape
    w = w.reshape(shape[:axis] + (N_KV_HEADS, GQA_GROUP, HEAD_DIM) + shape[axis + 1:])
    return jnp.swapaxes(w, axis, axis + 1).reshape(shape)


def kernel(x, p, w_in, w_pool, pool_scale, attn_sinks, w_out, ln1_g, ln1_b, w_ff1, w_ff2, ln2_g, ln2_b,
           w_ple, w_ple_gate, b_ple_gate):
    depth = w_in.shape[0]
    alpha = (2 * depth) ** 0.25
    bf = jnp.bfloat16
    row = lambda v: v.reshape(1, -1)
    q_cols = slice(POOL_WIDTH, POOL_WIDTH + ATTN_WIDTH)
    h = x
    for i in range(depth):
        w_in_i = jnp.concatenate(
            [w_in[i][:, :POOL_WIDTH], _pair_heads(w_in[i][:, q_cols], 1), w_in[i][:, q_cols.stop:]], axis=1)
        w_out_i = jnp.concatenate([w_out[i][:POOL_WIDTH], _pair_heads(w_out[i][q_cols], 0)], axis=0)
        h, w1, w2, wple, wgate = _mixer_call(
            h, attn_sinks[i], w_in_i.astype(bf), w_pool[i].astype(bf), row(pool_scale[i]), w_out_i.astype(bf),
            row(ln1_g[i]), row(ln1_b[i]), (w_ff1[i], w_ff2[i], w_ple[i], w_ple_gate[i]), alpha)
        h = _ffn_call(h, p[i], w1, w2, row(ln2_g[i]), row(ln2_b[i]), wple, wgate, row(b_ple_gate[i]), alpha)
    return h
```

```python
import functools
import math

import jax
import jax.numpy as jnp
from jax import lax
from jax.experimental import pallas as pl
from jax.experimental.pallas import tpu as pltpu

D_MODEL = 1024
PLE_DIM = 256
POOL_WIDTH = 512
POOL_WINDOWS = (2, 4, 8, 16)
POOL_GROUP = 128
HEAD_DIM = 64
ATTN_WIDTH = 512
N_Q_HEADS = 8
N_KV_HEADS = 2
GQA_GROUP = N_Q_HEADS // N_KV_HEADS
KV_WIDTH = N_KV_HEADS * HEAD_DIM
IN_WIDTH = POOL_WIDTH + ATTN_WIDTH + 2 * KV_WIDTH
BLOCK = 128
D_FF = 4 * D_MODEL
LN_EPS = 1e-5
NEG_INF = -1e30

POOL_HALO = 16
assert all(w & (w - 1) == 0 and w <= POOL_HALO for w in POOL_WINDOWS)
MIX_TILE = 1024
FFN_TILE = 1024
FF_CHUNK = 512
VMEM_LIMIT_BYTES = 56 * 1024 * 1024

assert N_KV_HEADS == 2 and 2 * HEAD_DIM == 128
PAIR_HEADS = tuple((j, GQA_GROUP + j) for j in range(GQA_GROUP))
HEAD_ORDER = tuple(h for pair in PAIR_HEADS for h in pair)


def _layer_norm(y, g, b):
    mu = jnp.mean(y, axis=-1, keepdims=True)
    yc = y - mu
    var = jnp.mean(yc * yc, axis=-1, keepdims=True)
    return yc * lax.rsqrt(var + LN_EPS) * g + b


def _const_spec(shape):
    return pl.BlockSpec(shape, lambda *_: (0,) * len(shape), pipeline_mode=pl.Buffered(1))


def _mixer_kernel(sinks_ref, x_ref, w_in_ref, w_pool_ref, pool_scale_ref, w_out_pool_ref, w_out_attn_ref,
                  g_ref, b_ref, wf1_ref, wf2_ref, wple_ref, wgate_ref,
                  o_ref, wf1_bf_ref, wf2_bf_ref, wple_bf_ref, wgate_bf_ref,
                  z_ref, mix_ref, halo_ref, kvprev_ref, logit_ref, w_pool_out_ref, *, alpha):
    tile = pl.program_id(1)
    ts = x_ref.shape[0]

    for src, dst in ((wf1_ref, wf1_bf_ref), (wf2_ref, wf2_bf_ref), (wple_ref, wple_bf_ref),
                     (wgate_ref, wgate_bf_ref)):
        dst[...] = src[...].astype(jnp.bfloat16)

    slot = tile % 2

    @pl.when(tile == 0)
    def _():
        halo_ref[0] = jnp.zeros(halo_ref.shape[1:], halo_ref.dtype)
        kvprev_ref[0] = jnp.zeros(kvprev_ref.shape[1:], kvprev_ref.dtype)

    qi = lax.broadcasted_iota(jnp.int32, (BLOCK, 2 * BLOCK), 0)
    kj = lax.broadcasted_iota(jnp.int32, (BLOCK, 2 * BLOCK), 1)
    dist = qi + BLOCK - kj
    in_band = (dist >= 0) & (dist < BLOCK)
    low_half = lax.broadcasted_iota(jnp.int32, (BLOCK, 2 * HEAD_DIM), 1) < HEAD_DIM

    @pl.when((pl.program_id(0) == 0) & (tile == 0))
    def _():
        dist_f = dist.astype(jnp.float32)
        for first in (0, 1):
            ok = in_band & (kj >= BLOCK) if first == 0 else in_band
            for h in range(N_Q_HEADS):
                slope = 2.0 ** (-8.0 * (h + 1) / N_Q_HEADS)
                masked = jnp.where(kj == qi, sinks_ref[h], NEG_INF)
                logit_ref[first, h] = jnp.where(ok, -slope * dist_f, masked)
        for g in range(len(POOL_WINDOWS)):
            cols = slice(g * POOL_GROUP, (g + 1) * POOL_GROUP)
            w_pool_out_ref[cols, :] = jnp.dot(
                w_pool_ref[g] * pool_scale_ref[:, cols], w_out_pool_ref[cols, :],
                precision=lax.Precision.HIGHEST, preferred_element_type=jnp.float32).astype(jnp.bfloat16)

    z_ref[...] = jnp.dot(x_ref[...].astype(jnp.bfloat16), w_in_ref[...], preferred_element_type=jnp.float32)

    halo = halo_ref[slot]
    for g, w in enumerate(POOL_WINDOWS):
        cols = slice(g * POOL_GROUP, (g + 1) * POOL_GROUP)
        u = z_ref[:, cols]
        s = jnp.concatenate([halo[:, cols], u], axis=0)
        shift = 1
        while shift < w:
            s = s + pltpu.roll(s, shift, axis=0)
            shift *= 2
        pos = tile * ts + lax.broadcasted_iota(jnp.int32, (POOL_HALO, POOL_GROUP), 0)
        count = jnp.minimum(pos + 1, w).astype(jnp.float32)
        mean = jnp.concatenate(
            [s[POOL_HALO:2 * POOL_HALO] / count, s[2 * POOL_HALO:] * (1.0 / w)], axis=0)
        mix_ref[:, cols] = mean - u
    halo_ref[1 - slot] = z_ref[ts - POOL_HALO:, :POOL_WIDTH]

    def attn_block(blk, prev):
        rows = slice(blk * BLOCK, (blk + 1) * BLOCK)
        table = jnp.minimum(tile * (ts // BLOCK) + blk, 1)
        valid = in_band & (kj >= BLOCK * (1 - table))

        cur = z_ref[rows, POOL_WIDTH + ATTN_WIDTH:].astype(jnp.bfloat16)
        band = jnp.concatenate([prev, cur], axis=0)
        q_parts = []
        for j, pair in enumerate(PAIR_HEADS):
            col = POOL_WIDTH + j * 2 * HEAD_DIM
            q = z_ref[rows, col:col + 2 * HEAD_DIM] * (1.0 / math.sqrt(HEAD_DIM))
            q_parts.append(jnp.where(low_half, q, 0.0).astype(jnp.bfloat16))
            q_parts.append(jnp.where(low_half, 0.0, q).astype(jnp.bfloat16))
        scores = lax.dot_general(jnp.concatenate(q_parts, axis=0), band[:, :KV_WIDTH],
                                 (((1,), (1,)), ((), ())),
                                 preferred_element_type=jnp.float32)
        p_parts, inv_parts = [], []
        for i, h in enumerate(HEAD_ORDER):
            sc = jnp.where(valid, scores[i * BLOCK:(i + 1) * BLOCK], 0.0) + logit_ref[table, h]
            e = jnp.exp(sc - jnp.max(sc, axis=-1, keepdims=True))
            inv_parts.append(1.0 / jnp.sum(e, axis=-1, keepdims=True))
            p_parts.append(jnp.where(valid, e, 0.0).astype(jnp.bfloat16))
        pv = jnp.dot(jnp.concatenate(p_parts, axis=0), band[:, KV_WIDTH:],
                     preferred_element_type=jnp.float32)
        for j in range(len(PAIR_HEADS)):
            lo = pv[2 * j * BLOCK:(2 * j + 1) * BLOCK] * inv_parts[2 * j]
            hi = pv[(2 * j + 1) * BLOCK:(2 * j + 2) * BLOCK] * inv_parts[2 * j + 1]
            col = POOL_WIDTH + j * 2 * HEAD_DIM
            mix_ref[rows, col:col + 2 * HEAD_DIM] = jnp.where(low_half, lo, hi)
        return cur

    prev = kvprev_ref[slot]
    for blk in range(ts // BLOCK):
        prev = attn_block(blk, prev)
    kvprev_ref[1 - slot] = prev

    mix = (jnp.dot(mix_ref[:, :POOL_WIDTH].astype(jnp.bfloat16), w_pool_out_ref[...],
                   preferred_element_type=jnp.float32)
           + jnp.dot(mix_ref[:, POOL_WIDTH:].astype(jnp.bfloat16), w_out_attn_ref[...],
                     preferred_element_type=jnp.float32))
    o_ref[...] = _layer_norm(alpha * x_ref[...] + mix, g_ref[...], b_ref[...])


def _ffn_kernel(h_ref, p_ref, w1_ref, w2_ref, g_ref, b_ref, wple_ref, wgate_ref, bgate_ref, o_ref, *,
                alpha):
    h = h_ref[...]
    hb = h.astype(jnp.bfloat16)
    acc = jnp.zeros(h.shape, jnp.float32)
    for c in range(D_FF // FF_CHUNK):
        cols = slice(c * FF_CHUNK, (c + 1) * FF_CHUNK)
        hid = jnp.maximum(jnp.dot(hb, w1_ref[:, cols], preferred_element_type=jnp.float32), 0.0)
        acc = acc + jnp.dot((hid * hid).astype(jnp.bfloat16), w2_ref[cols, :],
                            preferred_element_type=jnp.float32)
    h2 = _layer_norm(alpha * h + acc, g_ref[...], b_ref[...])
    gate = jax.nn.sigmoid(
        jnp.dot(h2.astype(jnp.bfloat16), wgate_ref[...], preferred_element_type=jnp.float32)
        + bgate_ref[...])
    ple = jnp.dot(p_ref[...].astype(jnp.bfloat16), wple_ref[...], preferred_element_type=jnp.float32)
    o_ref[...] = h2 + gate * ple


def _slab_spec(shape, axis, n_steps, tiles):
    assert shape[axis] % n_steps == 0 and (shape[axis] // n_steps) % (16 if axis == 0 else 128) == 0
    block = tuple(dim // n_steps if a == axis else dim for a, dim in enumerate(shape))
    return pl.BlockSpec(block, lambda b, i: tuple(b * tiles + i if a == axis else 0 for a in range(2)))


def _mixer_call(x, sinks, w_in, w_pool, pool_scale, w_out, w_out_attn, ln_g, ln_b, ffn_weights, alpha):
    batch, seq, d = x.shape
    assert d == D_MODEL and seq % MIX_TILE == 0 and MIX_TILE % BLOCK == 0
    tiles = seq // MIX_TILE
    slab_specs = [_slab_spec(w.shape, axis, batch * tiles, tiles) for w, axis in zip(ffn_weights, (1, 0, 0, 0))]
    grid_spec = pl.GridSpec(
        grid=(batch, tiles),
        in_specs=[
            pl.BlockSpec(memory_space=pltpu.SMEM),
            pl.BlockSpec((None, MIX_TILE, d), lambda b, i: (b, i, 0)),
            _const_spec((d, IN_WIDTH)),
            _const_spec((len(POOL_WINDOWS), POOL_GROUP, POOL_GROUP)),
            _const_spec((1, POOL_WIDTH)),
            _const_spec((POOL_WIDTH, d)),
            _const_spec((ATTN_WIDTH, d)),
            _const_spec((1, d)),
            _const_spec((1, d)),
            *slab_specs,
        ],
        out_specs=[pl.BlockSpec((None, MIX_TILE, d), lambda b, i: (b, i, 0)), *slab_specs],
        scratch_shapes=[
            pltpu.VMEM((MIX_TILE, IN_WIDTH), jnp.float32),
            pltpu.VMEM((MIX_TILE, POOL_WIDTH + ATTN_WIDTH), jnp.float32),
            pltpu.VMEM((2, POOL_HALO, POOL_WIDTH), jnp.float32),
            pltpu.VMEM((2, BLOCK, 2 * KV_WIDTH), jnp.bfloat16),
            pltpu.VMEM((2, N_Q_HEADS, BLOCK, 2 * BLOCK), jnp.float32),
            pltpu.VMEM((POOL_WIDTH, d), jnp.bfloat16),
        ],
    )
    return pl.pallas_call(
        functools.partial(_mixer_kernel, alpha=alpha),
        grid_spec=grid_spec,
        out_shape=[jax.ShapeDtypeStruct(x.shape, jnp.float32),
                   *(jax.ShapeDtypeStruct(w.shape, jnp.bfloat16) for w in ffn_weights)],
        compiler_params=pltpu.CompilerParams(
            dimension_semantics=("arbitrary", "arbitrary"), vmem_limit_bytes=VMEM_LIMIT_BYTES),
        name="mixer",
    )(sinks, x, w_in, w_pool, pool_scale, w_out, w_out_attn, ln_g, ln_b, *ffn_weights)


def _ffn_call(h, p, w1, w2, ln_g, ln_b, w_ple, w_gate, b_gate, alpha):
    batch, seq, d = h.shape
    assert seq % FFN_TILE == 0
    return pl.pallas_call(
        functools.partial(_ffn_kernel, alpha=alpha),
        grid=(batch, seq // FFN_TILE),
        in_specs=[
            pl.BlockSpec((None, FFN_TILE, d), lambda b, i: (b, i, 0)),
            pl.BlockSpec((None, FFN_TILE, PLE_DIM), lambda b, i: (b, i, 0)),
            _const_spec((d, D_FF)),
            _const_spec((D_FF, d)),
            _const_spec((1, d)),
            _const_spec((1, d)),
            _const_spec((PLE_DIM, d)),
            _const_spec((d, d)),
            _const_spec((1, d)),
        ],
        out_specs=pl.BlockSpec((None, FFN_TILE, d), lambda b, i: (b, i, 0)),
        out_shape=jax.ShapeDtypeStruct(h.shape, jnp.float32),
        compiler_params=pltpu.CompilerParams(
            dimension_semantics=("arbitrary", "arbitrary"), vmem_limit_bytes=VMEM_LIMIT_BYTES),
        name="ffn",
    )(h, p, w1, w2, ln_g, ln_b, w_ple, w_gate, b_gate)


def _pair_heads(w, axis):
    shape = w.shape
    w = w.reshape(shape[:axis] + (N_KV_HEADS, GQA_GROUP, HEAD_DIM) + shape[axis + 1:])
    return jnp.swapaxes(w, axis, axis + 1).reshape(shape)


def kernel(x, p, w_in, w_pool, pool_scale, attn_sinks, w_out, ln1_g, ln1_b, w_ff1, w_ff2, ln2_g, ln2_b,
           w_ple, w_ple_gate, b_ple_gate):
    depth = w_in.shape[0]
    alpha = (2 * depth) ** 0.25
    bf = jnp.bfloat16
    row = lambda v: v.reshape(1, -1)
    q_cols = slice(POOL_WIDTH, POOL_WIDTH + ATTN_WIDTH)
    h = x
    for i in range(depth):
        w_in_i = jnp.concatenate(
            [w_in[i][:, :POOL_WIDTH], _pair_heads(w_in[i][:, q_cols], 1), w_in[i][:, q_cols.stop:]], axis=1)
        w_out_attn = _pair_heads(w_out[i][q_cols], 0).astype(bf)
        h, w1, w2, wple, wgate = _mixer_call(
            h, attn_sinks[i], w_in_i.astype(bf), w_pool[i], row(pool_scale[i]), w_out[i], w_out_attn,
            row(ln1_g[i]), row(ln1_b[i]), (w_ff1[i], w_ff2[i], w_ple[i], w_ple_gate[i]), alpha)
        h = _ffn_call(h, p[i], w1, w2, row(ln2_g[i]), row(ln2_b[i]), wple, wgate, row(b_ple_gate[i]), alpha)
    return h
```

```python
import functools
import math

import jax
import jax.numpy as jnp
from jax import lax
from jax.experimental import pallas as pl
from jax.experimental.pallas import tpu as pltpu

D_MODEL = 1024
PLE_DIM = 256
POOL_WIDTH = 512
POOL_WINDOWS = (2, 4, 8, 16)
POOL_GROUP = 128
HEAD_DIM = 64
ATTN_WIDTH = 512
N_Q_HEADS = 8
N_KV_HEADS = 2
GQA_GROUP = N_Q_HEADS // N_KV_HEADS
KV_WIDTH = N_KV_HEADS * HEAD_DIM
IN_WIDTH = POOL_WIDTH + ATTN_WIDTH + 2 * KV_WIDTH
BLOCK = 128
D_FF = 4 * D_MODEL
LN_EPS = 1e-5
NEG_INF = -1e30
LOG2_E = math.log2(math.e)

POOL_HALO = 16
assert all(w & (w - 1) == 0 and w <= POOL_HALO for w in POOL_WINDOWS)
MIX_TILE = 1024
FFN_TILE = 1024
FF_CHUNK = 512
VMEM_LIMIT_BYTES = 56 * 1024 * 1024

assert N_KV_HEADS == 2 and 2 * HEAD_DIM == 128
PAIR_HEADS = tuple((j, GQA_GROUP + j) for j in range(GQA_GROUP))
HEAD_ORDER = tuple(h for pair in PAIR_HEADS for h in pair)


def _layer_norm(y, g, b):
    mu = jnp.mean(y, axis=-1, keepdims=True)
    yc = y - mu
    var = jnp.mean(yc * yc, axis=-1, keepdims=True)
    return yc * lax.rsqrt(var + LN_EPS) * g + b


def _const_spec(shape):
    return pl.BlockSpec(shape, lambda *_: (0,) * len(shape), pipeline_mode=pl.Buffered(1))


def _mixer_kernel(sinks_ref, x_ref, w_in_ref, w_pool_ref, pool_scale_ref, w_out_ref, g_ref, b_ref,
                  wf1_ref, wf2_ref, wple_ref, wgate_ref,
                  o_ref, wf1_bf_ref, wf2_bf_ref, wple_bf_ref, wgate_bf_ref,
                  z_ref, mix_ref, halo_ref, kvprev_ref, logit_ref, *, alpha):
    tile = pl.program_id(1)
    ts = x_ref.shape[0]

    for src, dst in ((wf1_ref, wf1_bf_ref), (wf2_ref, wf2_bf_ref), (wple_ref, wple_bf_ref),
                     (wgate_ref, wgate_bf_ref)):
        dst[...] = src[...].astype(jnp.bfloat16)

    slot = tile % 2

    @pl.when(tile == 0)
    def _():
        halo_ref[0] = jnp.zeros(halo_ref.shape[1:], halo_ref.dtype)
        kvprev_ref[0] = jnp.zeros(kvprev_ref.shape[1:], kvprev_ref.dtype)

    qi = lax.broadcasted_iota(jnp.int32, (BLOCK, 2 * BLOCK), 0)
    kj = lax.broadcasted_iota(jnp.int32, (BLOCK, 2 * BLOCK), 1)
    dist = qi + BLOCK - kj
    in_band = (dist >= 0) & (dist < BLOCK)
    low_half = lax.broadcasted_iota(jnp.int32, (BLOCK, 2 * HEAD_DIM), 1) < HEAD_DIM

    @pl.when((pl.program_id(0) == 0) & (tile == 0))
    def _():
        dist_f = dist.astype(jnp.float32)
        for first in (0, 1):
            ok = in_band & (kj >= BLOCK) if first == 0 else in_band
            for h in range(N_Q_HEADS):
                slope = 2.0 ** (-8.0 * (h + 1) / N_Q_HEADS)
                masked = jnp.where(kj == qi, sinks_ref[h:h + 1, :] * LOG2_E, NEG_INF)
                logit_ref[first, h] = jnp.where(ok, (-slope * LOG2_E) * dist_f, masked)

    z_ref[...] = jnp.dot(x_ref[...].astype(jnp.bfloat16), w_in_ref[...], preferred_element_type=jnp.float32)

    halo = halo_ref[slot]
    for g, w in enumerate(POOL_WINDOWS):
        cols = slice(g * POOL_GROUP, (g + 1) * POOL_GROUP)
        u = z_ref[:, cols]
        s = jnp.concatenate([halo[:, cols], u], axis=0)
        shift = 1
        while shift < w:
            s = s + pltpu.roll(s, shift, axis=0)
            shift *= 2
        pos = tile * ts + lax.broadcasted_iota(jnp.int32, (POOL_HALO, POOL_GROUP), 0)
        count = jnp.minimum(pos + 1, w).astype(jnp.float32)
        mean = jnp.concatenate(
            [s[POOL_HALO:2 * POOL_HALO] / count, s[2 * POOL_HALO:] * (1.0 / w)], axis=0)
        y = jnp.dot((mean - u).astype(jnp.bfloat16), w_pool_ref[g], preferred_element_type=jnp.float32)
        mix_ref[:, cols] = y * pool_scale_ref[:, cols]
    halo_ref[1 - slot] = z_ref[ts - POOL_HALO:, :POOL_WIDTH]

    def attn_block(blk, prev):
        rows = slice(blk * BLOCK, (blk + 1) * BLOCK)
        table = jnp.minimum(tile * (ts // BLOCK) + blk, 1)
        valid = in_band & (kj >= BLOCK * (1 - table))

        cur = z_ref[rows, POOL_WIDTH + ATTN_WIDTH:].astype(jnp.bfloat16)
        band = jnp.concatenate([prev, cur], axis=0)
        q_parts = []
        for j, pair in enumerate(PAIR_HEADS):
            col = POOL_WIDTH + j * 2 * HEAD_DIM
            q = z_ref[rows, col:col + 2 * HEAD_DIM] * (LOG2_E / math.sqrt(HEAD_DIM))
            q_parts.append(jnp.where(low_half, q, 0.0).astype(jnp.bfloat16))
            q_parts.append(jnp.where(low_half, 0.0, q).astype(jnp.bfloat16))
        scores = lax.dot_general(jnp.concatenate(q_parts, axis=0), band[:, :KV_WIDTH],
                                 (((1,), (1,)), ((), ())),
                                 preferred_element_type=jnp.float32)
        p_parts, inv_parts = [], []
        for i, h in enumerate(HEAD_ORDER):
            sc = jnp.where(valid, scores[i * BLOCK:(i + 1) * BLOCK], 0.0) + logit_ref[table, h]
            e = jnp.exp2(sc - jnp.max(sc, axis=-1, keepdims=True))
            inv_parts.append(1.0 / jnp.sum(e, axis=-1, keepdims=True))
            p_parts.append(jnp.where(valid, e, 0.0).astype(jnp.bfloat16))
        pv = jnp.dot(jnp.concatenate(p_parts, axis=0), band[:, KV_WIDTH:],
                     preferred_element_type=jnp.float32)
        for j in range(len(PAIR_HEADS)):
            lo = pv[2 * j * BLOCK:(2 * j + 1) * BLOCK] * inv_parts[2 * j]
            hi = pv[(2 * j + 1) * BLOCK:(2 * j + 2) * BLOCK] * inv_parts[2 * j + 1]
            col = POOL_WIDTH + j * 2 * HEAD_DIM
            mix_ref[rows, col:col + 2 * HEAD_DIM] = jnp.where(low_half, lo, hi)
        return cur

    prev = kvprev_ref[slot]
    for blk in range(ts // BLOCK):
        prev = attn_block(blk, prev)
    kvprev_ref[1 - slot] = prev

    mix = jnp.dot(mix_ref[...].astype(jnp.bfloat16), w_out_ref[...], preferred_element_type=jnp.float32)
    o_ref[...] = _layer_norm(alpha * x_ref[...] + mix, g_ref[...], b_ref[...])


def _ffn_kernel(h_ref, p_ref, w1_ref, w2_ref, g_ref, b_ref, wple_ref, wgate_ref, bgate_ref, o_ref, *,
                alpha):
    h = h_ref[...]
    hb = h.astype(jnp.bfloat16)
    acc = jnp.zeros(h.shape, jnp.float32)
    for c in range(D_FF // FF_CHUNK):
        cols = slice(c * FF_CHUNK, (c + 1) * FF_CHUNK)
        hid = jnp.maximum(jnp.dot(hb, w1_ref[:, cols], preferred_element_type=jnp.float32), 0.0)
        acc = acc + jnp.dot((hid * hid).astype(jnp.bfloat16), w2_ref[cols, :],
                            preferred_element_type=jnp.float32)
    h2 = _layer_norm(alpha * h + acc, g_ref[...], b_ref[...])
    gate = jax.nn.sigmoid(
        jnp.dot(h2.astype(jnp.bfloat16), wgate_ref[...], preferred_element_type=jnp.float32)
        + bgate_ref[...])
    ple = jnp.dot(p_ref[...].astype(jnp.bfloat16), wple_ref[...], preferred_element_type=jnp.float32)
    o_ref[...] = h2 + gate * ple


def _slab_spec(shape, axis, n_steps, tiles):
    assert shape[axis] % n_steps == 0 and (shape[axis] // n_steps) % (16 if axis == 0 else 128) == 0
    block = tuple(dim // n_steps if a == axis else dim for a, dim in enumerate(shape))
    return pl.BlockSpec(block, lambda b, i: tuple(b * tiles + i if a == axis else 0 for a in range(2)))


def _mixer_call(x, sinks, w_in, w_pool, pool_scale, w_out, ln_g, ln_b, ffn_weights, alpha):
    batch, seq, d = x.shape
    assert d == D_MODEL and seq % MIX_TILE == 0 and MIX_TILE % BLOCK == 0
    tiles = seq // MIX_TILE
    slab_specs = [_slab_spec(w.shape, axis, batch * tiles, tiles) for w, axis in zip(ffn_weights, (1, 0, 0, 0))]
    grid_spec = pl.GridSpec(
        grid=(batch, tiles),
        in_specs=[
            _const_spec((N_Q_HEADS, 2 * BLOCK)),
            pl.BlockSpec((None, MIX_TILE, d), lambda b, i: (b, i, 0)),
            _const_spec((d, IN_WIDTH)),
            _const_spec((len(POOL_WINDOWS), POOL_GROUP, POOL_GROUP)),
            _const_spec((1, POOL_WIDTH)),
            _const_spec((POOL_WIDTH + ATTN_WIDTH, d)),
            _const_spec((1, d)),
            _const_spec((1, d)),
            *slab_specs,
        ],
        out_specs=[pl.BlockSpec((None, MIX_TILE, d), lambda b, i: (b, i, 0)), *slab_specs],
        scratch_shapes=[
            pltpu.VMEM((MIX_TILE, IN_WIDTH), jnp.float32),
            pltpu.VMEM((MIX_TILE, POOL_WIDTH + ATTN_WIDTH), jnp.float32),
            pltpu.VMEM((2, POOL_HALO, POOL_WIDTH), jnp.float32),
            pltpu.VMEM((2, BLOCK, 2 * KV_WIDTH), jnp.bfloat16),
            pltpu.VMEM((2, N_Q_HEADS, BLOCK, 2 * BLOCK), jnp.float32),
        ],
    )
    return pl.pallas_call(
        functools.partial(_mixer_kernel, alpha=alpha),
        grid_spec=grid_spec,
        out_shape=[jax.ShapeDtypeStruct(x.shape, jnp.float32),
                   *(jax.ShapeDtypeStruct(w.shape, jnp.bfloat16) for w in ffn_weights)],
        compiler_params=pltpu.CompilerParams(
            dimension_semantics=("arbitrary", "arbitrary"), vmem_limit_bytes=VMEM_LIMIT_BYTES),
        name="mixer",
    )(jnp.broadcast_to(sinks[:, None], (N_Q_HEADS, 2 * BLOCK)), x, w_in, w_pool, pool_scale, w_out, ln_g,
      ln_b, *ffn_weights)


def _ffn_call(h, p, w1, w2, ln_g, ln_b, w_ple, w_gate, b_gate, alpha):
    batch, seq, d = h.shape
    assert seq % FFN_TILE == 0
    return pl.pallas_call(
        functools.partial(_ffn_kernel, alpha=alpha),
        grid=(batch, seq // FFN_TILE),
        in_specs=[
            pl.BlockSpec((None, FFN_TILE, d), lambda b, i: (b, i, 0)),
            pl.BlockSpec((None, FFN_TILE, PLE_DIM), lambda b, i: (b, i, 0)),
            _const_spec((d, D_FF)),
            _const_spec((D_FF, d)),
            _const_spec((1, d)),
            _const_spec((1, d)),
            _const_spec((PLE_DIM, d)),
            _const_spec((d, d)),
            _const_spec((1, d)),
        ],
        out_specs=pl.BlockSpec((None, FFN_TILE, d), lambda b, i: (b, i, 0)),
        out_shape=jax.ShapeDtypeStruct(h.shape, jnp.float32),
        compiler_params=pltpu.CompilerParams(
            dimension_semantics=("arbitrary", "arbitrary"), vmem_limit_bytes=VMEM_LIMIT_BYTES),
        name="ffn",
    )(h, p, w1, w2, ln_g, ln_b, w_ple, w_gate, b_gate)


def _pair_heads(w, axis):
    shape = w.shape
    w = w.reshape(shape[:axis] + (N_KV_HEADS, GQA_GROUP, HEAD_DIM) + shape[axis + 1:])
    return jnp.swapaxes(w, axis, axis + 1).reshape(shape)


def kernel(x, p, w_in, w_pool, pool_scale, attn_sinks, w_out, ln1_g, ln1_b, w_ff1, w_ff2, ln2_g, ln2_b,
           w_ple, w_ple_gate, b_ple_gate):
    depth = w_in.shape[0]
    alpha = (2 * depth) ** 0.25
    bf = jnp.bfloat16
    row = lambda v: v.reshape(1, -1)
    q_cols = slice(POOL_WIDTH, POOL_WIDTH + ATTN_WIDTH)
    h = x
    for i in range(depth):
        w_in_i = jnp.concatenate(
            [w_in[i][:, :POOL_WIDTH], _pair_heads(w_in[i][:, q_cols], 1), w_in[i][:, q_cols.stop:]], axis=1)
        w_out_i = jnp.concatenate([w_out[i][:POOL_WIDTH], _pair_heads(w_out[i][q_cols], 0)], axis=0)
        h, w1, w2, wple, wgate = _mixer_call(
            h, attn_sinks[i], w_in_i.astype(bf), w_pool[i].astype(bf), row(pool_scale[i]), w_out_i.astype(bf),
            row(ln1_g[i]), row(ln1_b[i]), (w_ff1[i], w_ff2[i], w_ple[i], w_ple_gate[i]), alpha)
        h = _ffn_call(h, p[i], w1, w2, row(ln2_g[i]), row(ln2_b[i]), wple, wgate, row(b_ple_gate[i]), alpha)
    return h
```

```python
import functools
import math

import jax
import jax.numpy as jnp
from jax import lax
from jax.experimental import pallas as pl
from jax.experimental.pallas import tpu as pltpu

D_MODEL = 1024
PLE_DIM = 256
POOL_WIDTH = 512
POOL_WINDOWS = (2, 4, 8, 16)
POOL_GROUP = 128
HEAD_DIM = 64
ATTN_WIDTH = 512
N_Q_HEADS = 8
N_KV_HEADS = 2
GQA_GROUP = N_Q_HEADS // N_KV_HEADS
KV_WIDTH = N_KV_HEADS * HEAD_DIM
IN_WIDTH = POOL_WIDTH + ATTN_WIDTH + 2 * KV_WIDTH
BLOCK = 128
D_FF = 4 * D_MODEL
LN_EPS = 1e-5
NEG_INF = -1e30
LOG2_E = math.log2(math.e)

POOL_HALO = 16
assert all(w & (w - 1) == 0 and w <= POOL_HALO for w in POOL_WINDOWS)
MIX_TILE = 1024
FFN_TILE = 1024
FF_CHUNK = 512
VMEM_LIMIT_BYTES = 56 * 1024 * 1024

assert N_KV_HEADS == 2 and 2 * HEAD_DIM == 128
PAIR_HEADS = tuple((j, GQA_GROUP + j) for j in range(GQA_GROUP))
HEAD_ORDER = tuple(h for pair in PAIR_HEADS for h in pair)


def _layer_norm(y, g, b):
    mu = jnp.mean(y, axis=-1, keepdims=True)
    yc = y - mu
    var = jnp.mean(yc * yc, axis=-1, keepdims=True)
    return yc * lax.rsqrt(var + LN_EPS) * g + b


def _const_spec(shape):
    return pl.BlockSpec(shape, lambda *_: (0,) * len(shape), pipeline_mode=pl.Buffered(1))


def _mixer_kernel(sinks_ref, x_ref, w_in_ref, w_pool_ref, pool_scale_ref, w_out_ref, g_ref, b_ref,
                  wf1_ref, wf2_ref, wple_ref, wgate_ref,
                  o_ref, wf1_bf_ref, wf2_bf_ref, wple_bf_ref, wgate_bf_ref,
                  z_ref, mix_ref, halo_ref, kvprev_ref, logit_ref, *, alpha):
    tile = pl.program_id(1)
    ts = x_ref.shape[0]

    for src, dst in ((wf1_ref, wf1_bf_ref), (wf2_ref, wf2_bf_ref), (wple_ref, wple_bf_ref),
                     (wgate_ref, wgate_bf_ref)):
        dst[...] = src[...].astype(jnp.bfloat16)

    slot = tile % 2

    @pl.when(tile == 0)
    def _():
        halo_ref[0] = jnp.zeros(halo_ref.shape[1:], halo_ref.dtype)
        kvprev_ref[0] = jnp.zeros(kvprev_ref.shape[1:], kvprev_ref.dtype)

    qi = lax.broadcasted_iota(jnp.int32, (BLOCK, 2 * BLOCK), 0)
    kj = lax.broadcasted_iota(jnp.int32, (BLOCK, 2 * BLOCK), 1)
    dist = qi + BLOCK - kj
    in_band = (dist >= 0) & (dist < BLOCK)
    low_half = lax.broadcasted_iota(jnp.int32, (BLOCK, 2 * HEAD_DIM), 1) < HEAD_DIM

    @pl.when((pl.program_id(0) == 0) & (tile == 0))
    def _():
        dist_f = dist.astype(jnp.float32)
        for first in (0, 1):
            ok = in_band & (kj >= BLOCK) if first == 0 else in_band
            for h in range(N_Q_HEADS):
                slope = 2.0 ** (-8.0 * (h + 1) / N_Q_HEADS)
                masked = jnp.where(kj == qi, sinks_ref[h] * LOG2_E, NEG_INF)
                logit_ref[first, h] = jnp.where(ok, (-slope * LOG2_E) * dist_f, masked)

    z_ref[...] = jnp.dot(x_ref[...].astype(jnp.bfloat16), w_in_ref[...], preferred_element_type=jnp.float32)

    halo = halo_ref[slot]
    for g, w in enumerate(POOL_WINDOWS):
        cols = slice(g * POOL_GROUP, (g + 1) * POOL_GROUP)
        u = z_ref[:, cols]
        s = jnp.concatenate([halo[:, cols], u], axis=0)
        shift = 1
        while shift < w:
            s = s + pltpu.roll(s, shift, axis=0)
            shift *= 2
        pos = tile * ts + lax.broadcasted_iota(jnp.int32, (POOL_HALO, POOL_GROUP), 0)
        count = jnp.minimum(pos + 1, w).astype(jnp.float32)
        mean = jnp.concatenate(
            [s[POOL_HALO:2 * POOL_HALO] / count, s[2 * POOL_HALO:] * (1.0 / w)], axis=0)
        y = jnp.dot((mean - u).astype(jnp.bfloat16), w_pool_ref[g].astype(jnp.bfloat16),
                    preferred_element_type=jnp.float32)
        mix_ref[:, cols] = y * pool_scale_ref[:, cols]
    halo_ref[1 - slot] = z_ref[ts - POOL_HALO:, :POOL_WIDTH]

    def attn_block(blk, prev):
        rows = slice(blk * BLOCK, (blk + 1) * BLOCK)
        table = jnp.minimum(tile * (ts // BLOCK) + blk, 1)
        valid = in_band & (kj >= BLOCK * (1 - table))

        cur = z_ref[rows, POOL_WIDTH + ATTN_WIDTH:].astype(jnp.bfloat16)
        band = jnp.concatenate([prev, cur], axis=0)
        q_parts = []
        for j, pair in enumerate(PAIR_HEADS):
            col = POOL_WIDTH + j * 2 * HEAD_DIM
            q = z_ref[rows, col:col + 2 * HEAD_DIM] * (LOG2_E / math.sqrt(HEAD_DIM))
            q_parts.append(jnp.where(low_half, q, 0.0).astype(jnp.bfloat16))
            q_parts.append(jnp.where(low_half, 0.0, q).astype(jnp.bfloat16))
        scores = lax.dot_general(jnp.concatenate(q_parts, axis=0), band[:, :KV_WIDTH],
                                 (((1,), (1,)), ((), ())),
                                 preferred_element_type=jnp.float32)
        p_parts, inv_parts = [], []
        for i, h in enumerate(HEAD_ORDER):
            sc = jnp.where(valid, scores[i * BLOCK:(i + 1) * BLOCK], 0.0) + logit_ref[table, h]
            e = jnp.exp2(sc - jnp.max(sc, axis=-1, keepdims=True))
            inv_parts.append(1.0 / jnp.sum(e, axis=-1, keepdims=True))
            p_parts.append(jnp.where(valid, e, 0.0).astype(jnp.bfloat16))
        pv = jnp.dot(jnp.concatenate(p_parts, axis=0), band[:, KV_WIDTH:],
                     preferred_element_type=jnp.float32)
        for j in range(len(PAIR_HEADS)):
            lo = pv[2 * j * BLOCK:(2 * j + 1) * BLOCK] * inv_parts[2 * j]
            hi = pv[(2 * j + 1) * BLOCK:(2 * j + 2) * BLOCK] * inv_parts[2 * j + 1]
            col = POOL_WIDTH + j * 2 * HEAD_DIM
            mix_ref[rows, col:col + 2 * HEAD_DIM] = jnp.where(low_half, lo, hi)
        return cur

    prev = kvprev_ref[slot]
    for blk in range(ts // BLOCK):
        prev = attn_block(blk, prev)
    kvprev_ref[1 - slot] = prev

    mix = jnp.dot(mix_ref[...].astype(jnp.bfloat16), w_out_ref[...], preferred_element_type=jnp.float32)
    o_ref[...] = _layer_norm(alpha * x_ref[...] + mix, g_ref[...], b_ref[...])


def _ffn_kernel(h_ref, p_ref, w1_ref, w2_ref, g_ref, b_ref, wple_ref, wgate_ref, bgate_ref, o_ref, *,
                alpha):
    h = h_ref[...]
    hb = h.astype(jnp.bfloat16)
    acc = jnp.zeros(h.shape, jnp.float32)
    for c in range(D_FF // FF_CHUNK):
        cols = slice(c * FF_CHUNK, (c + 1) * FF_CHUNK)
        hid = jnp.maximum(jnp.dot(hb, w1_ref[:, cols], preferred_element_type=jnp.float32), 0.0)
        acc = acc + jnp.dot((hid * hid).astype(jnp.bfloat16), w2_ref[cols, :],
                            preferred_element_type=jnp.float32)
    h2 = _layer_norm(alpha * h + acc, g_ref[...], b_ref[...])
    gate = jax.nn.sigmoid(
        jnp.dot(h2.astype(jnp.bfloat16), wgate_ref[...], preferred_element_type=jnp.float32)
        + bgate_ref[...])
    ple = jnp.dot(p_ref[...].astype(jnp.bfloat16), wple_ref[...], preferred_element_type=jnp.float32)
    o_ref[...] = h2 + gate * ple


def _slab_spec(shape, axis, n_steps, tiles):
    assert shape[axis] % n_steps == 0 and (shape[axis] // n_steps) % (16 if axis == 0 else 128) == 0
    block = tuple(dim // n_steps if a == axis else dim for a, dim in enumerate(shape))
    return pl.BlockSpec(block, lambda b, i: tuple(b * tiles + i if a == axis else 0 for a in range(2)))


def _mixer_call(x, sinks, w_in, w_pool, pool_scale, w_out, ln_g, ln_b, ffn_weights, alpha):
    batch, seq, d = x.shape
    assert d == D_MODEL and seq % MIX_TILE == 0 and MIX_TILE % BLOCK == 0
    tiles = seq // MIX_TILE
    slab_specs = [_slab_spec(w.shape, axis, batch * tiles, tiles) for w, axis in zip(ffn_weights, (1, 0, 0, 0))]
    grid_spec = pl.GridSpec(
        grid=(batch, tiles),
        in_specs=[
            pl.BlockSpec(memory_space=pltpu.SMEM),
            pl.BlockSpec((None, MIX_TILE, d), lambda b, i: (b, i, 0)),
            _const_spec((d, IN_WIDTH)),
            _const_spec((len(POOL_WINDOWS), POOL_GROUP, POOL_GROUP)),
            _const_spec((1, POOL_WIDTH)),
            _const_spec((POOL_WIDTH + ATTN_WIDTH, d)),
            _const_spec((1, d)),
            _const_spec((1, d)),
            *slab_specs,
        ],
        out_specs=[pl.BlockSpec((None, MIX_TILE, d), lambda b, i: (b, i, 0)), *slab_specs],
        scratch_shapes=[
            pltpu.VMEM((MIX_TILE, IN_WIDTH), jnp.float32),
            pltpu.VMEM((MIX_TILE, POOL_WIDTH + ATTN_WIDTH), jnp.float32),
            pltpu.VMEM((2, POOL_HALO, POOL_WIDTH), jnp.float32),
            pltpu.VMEM((2, BLOCK, 2 * KV_WIDTH), jnp.bfloat16),
            pltpu.VMEM((2, N_Q_HEADS, BLOCK, 2 * BLOCK), jnp.float32),
        ],
    )
    return pl.pallas_call(
        functools.partial(_mixer_kernel, alpha=alpha),
        grid_spec=grid_spec,
        out_shape=[jax.ShapeDtypeStruct(x.shape, jnp.float32),
                   *(jax.ShapeDtypeStruct(w.shape, jnp.bfloat16) for w in ffn_weights)],
        compiler_params=pltpu.CompilerParams(
            dimension_semantics=("arbitrary", "arbitrary"), vmem_limit_bytes=VMEM_LIMIT_BYTES),
        name="mixer",
    )(sinks, x, w_in, w_pool, pool_scale, w_out, ln_g, ln_b, *ffn_weights)


def _ffn_call(h, p, w1, w2, ln_g, ln_b, w_ple, w_gate, b_gate, alpha):
    batch, seq, d = h.shape
    assert seq % FFN_TILE == 0
    return pl.pallas_call(
        functools.partial(_ffn_kernel, alpha=alpha),
        grid=(batch, seq // FFN_TILE),
        in_specs=[
            pl.BlockSpec((None, FFN_TILE, d), lambda b, i: (b, i, 0)),
            pl.BlockSpec((None, FFN_TILE, PLE_DIM), lambda b, i: (b, i, 0)),
            _const_spec((d, D_FF)),
            _const_spec((D_FF, d)),
            _const_spec((1, d)),
            _const_spec((1, d)),
            _const_spec((PLE_DIM, d)),
            _const_spec((d, d)),
            _const_spec((1, d)),
        ],
        out_specs=pl.BlockSpec((None, FFN_TILE, d), lambda b, i: (b, i, 0)),
        out_shape=jax.ShapeDtypeStruct(h.shape, jnp.float32),
        compiler_params=pltpu.CompilerParams(
            dimension_semantics=("arbitrary", "arbitrary"), vmem_limit_bytes=VMEM_LIMIT_BYTES),
        name="ffn",
    )(h, p, w1, w2, ln_g, ln_b, w_ple, w_gate, b_gate)


def _pair_heads(w, axis):
    shape = w.shape
    w = w.reshape(shape[:axis] + (N_KV_HEADS, GQA_GROUP, HEAD_DIM) + shape[axis + 1:])
    return jnp.swapaxes(w, axis, axis + 1).reshape(shape)


def kernel(x, p, w_in, w_pool, pool_scale, attn_sinks, w_out, ln1_g, ln1_b, w_ff1, w_ff2, ln2_g, ln2_b,
           w_ple, w_ple_gate, b_ple_gate):
    depth = w_in.shape[0]
    alpha = (2 * depth) ** 0.25
    bf = jnp.bfloat16
    row = lambda v: v.reshape(1, -1)
    q_cols = slice(POOL_WIDTH, POOL_WIDTH + ATTN_WIDTH)
    h = x
    for i in range(depth):
        w_in_i = jnp.concatenate(
            [w_in[i][:, :POOL_WIDTH], _pair_heads(w_in[i][:, q_cols], 1), w_in[i][:, q_cols.stop:]], axis=1)
        w_out_i = jnp.concatenate([w_out[i][:POOL_WIDTH], _pair_heads(w_out[i][q_cols], 0)], axis=0)
        h, w1, w2, wple, wgate = _mixer_call(
            h, attn_sinks[i], w_in_i.astype(bf), w_pool[i], row(pool_scale[i]), w_out_i.astype(bf),
            row(ln1_g[i]), row(ln1_b[i]), (w_ff1[i], w_ff2[i], w_ple[i], w_ple_gate[i]), alpha)
        h = _ffn_call(h, p[i], w1, w2, row(ln2_g[i]), row(ln2_b[i]), wple, wgate, row(b_ple_gate[i]), alpha)
    return h
```

```python
import functools
import math

import jax
import jax.numpy as jnp
from jax import lax
from jax.experimental import pallas as pl
from jax.experimental.pallas import tpu as pltpu

D_MODEL = 1024
PLE_DIM = 256
POOL_WIDTH = 512
POOL_WINDOWS = (2, 4, 8, 16)
POOL_GROUP = 128
HEAD_DIM = 64
ATTN_WIDTH = 512
N_Q_HEADS = 8
N_KV_HEADS = 2
GQA_GROUP = N_Q_HEADS // N_KV_HEADS
KV_WIDTH = N_KV_HEADS * HEAD_DIM
IN_WIDTH = POOL_WIDTH + ATTN_WIDTH + 2 * KV_WIDTH
BLOCK = 128
D_FF = 4 * D_MODEL
LN_EPS = 1e-5
NEG_INF = -1e30
LOG2_E = math.log2(math.e)

POOL_HALO = 16
assert all(w & (w - 1) == 0 and w <= POOL_HALO for w in POOL_WINDOWS)
MIX_TILE = 1024
FFN_TILE = 1024
FF_CHUNK = 512
VMEM_LIMIT_BYTES = 56 * 1024 * 1024
BF16_TILE = (16, 128)

assert N_KV_HEADS == 2 and 2 * HEAD_DIM == 128
PAIR_HEADS = tuple((j, GQA_GROUP + j) for j in range(GQA_GROUP))
HEAD_ORDER = tuple(h for pair in PAIR_HEADS for h in pair)


def _layer_norm(y, g, b):
    mu = jnp.mean(y, axis=-1, keepdims=True)
    yc = y - mu
    var = jnp.mean(yc * yc, axis=-1, keepdims=True)
    return yc * lax.rsqrt(var + LN_EPS) * g + b


def _const_spec(shape):
    return pl.BlockSpec(shape, lambda *_: (0,) * len(shape), pipeline_mode=pl.Buffered(1))


def _mixer_kernel(sinks_ref, x_ref, w_in_ref, w_pool_ref, pool_scale_ref, w_out_ref, g_ref, b_ref,
                  wf1_ref, wf2_ref, wple_ref, wgate_ref,
                  o_ref, wf1_bf_ref, wf2_bf_ref, wple_bf_ref, wgate_bf_ref,
                  z_ref, mix_ref, halo_ref, kvprev_ref, logit_ref, *, alpha):
    tile = pl.program_id(1)
    ts = x_ref.shape[0]

    for src, dst in ((wf1_ref, wf1_bf_ref), (wf2_ref, wf2_bf_ref), (wple_ref, wple_bf_ref),
                     (wgate_ref, wgate_bf_ref)):
        dst[...] = src[...].astype(jnp.bfloat16)

    slot = tile % 2

    @pl.when(tile == 0)
    def _():
        halo_ref[0] = jnp.zeros(halo_ref.shape[1:], halo_ref.dtype)
        kvprev_ref[0] = jnp.zeros(kvprev_ref.shape[1:], kvprev_ref.dtype)

    qi = lax.broadcasted_iota(jnp.int32, (BLOCK, 2 * BLOCK), 0)
    kj = lax.broadcasted_iota(jnp.int32, (BLOCK, 2 * BLOCK), 1)
    dist = qi + BLOCK - kj
    in_band = (dist >= 0) & (dist < BLOCK)
    low_half = lax.broadcasted_iota(jnp.int32, (BLOCK, 2 * HEAD_DIM), 1) < HEAD_DIM

    @pl.when((pl.program_id(0) == 0) & (tile == 0))
    def _():
        dist_f = dist.astype(jnp.float32)
        for first in (0, 1):
            ok = in_band & (kj >= BLOCK) if first == 0 else in_band
            for h in range(N_Q_HEADS):
                slope = 2.0 ** (-8.0 * (h + 1) / N_Q_HEADS)
                masked = jnp.where(kj == qi, sinks_ref[h] * LOG2_E, NEG_INF)
                logit_ref[first, h] = jnp.where(ok, (-slope * LOG2_E) * dist_f, masked)

    z_ref[...] = jnp.dot(x_ref[...].astype(jnp.bfloat16), w_in_ref[...], preferred_element_type=jnp.float32)

    halo = halo_ref[slot]
    for g, w in enumerate(POOL_WINDOWS):
        cols = slice(g * POOL_GROUP, (g + 1) * POOL_GROUP)
        u = z_ref[:, cols]
        s = jnp.concatenate([halo[:, cols], u], axis=0)
        shift = 1
        while shift < w:
            s = s + pltpu.roll(s, shift, axis=0)
            shift *= 2
        pos = tile * ts + lax.broadcasted_iota(jnp.int32, (POOL_HALO, POOL_GROUP), 0)
        count = jnp.minimum(pos + 1, w).astype(jnp.float32)
        mean = jnp.concatenate(
            [s[POOL_HALO:2 * POOL_HALO] / count, s[2 * POOL_HALO:] * (1.0 / w)], axis=0)
        y = jnp.dot((mean - u).astype(jnp.bfloat16), w_pool_ref[g].astype(jnp.bfloat16),
                    preferred_element_type=jnp.float32)
        mix_ref[:, cols] = y * pool_scale_ref[:, cols]
    halo_ref[1 - slot] = z_ref[ts - POOL_HALO:, :POOL_WIDTH]

    def attn_block(blk, prev):
        rows = slice(blk * BLOCK, (blk + 1) * BLOCK)
        table = jnp.minimum(tile * (ts // BLOCK) + blk, 1)
        valid = in_band & (kj >= BLOCK * (1 - table))

        cur = z_ref[rows, POOL_WIDTH + ATTN_WIDTH:].astype(jnp.bfloat16)
        band = jnp.concatenate([prev, cur], axis=0)
        q_parts = []
        for j, pair in enumerate(PAIR_HEADS):
            col = POOL_WIDTH + j * 2 * HEAD_DIM
            q = z_ref[rows, col:col + 2 * HEAD_DIM] * (LOG2_E / math.sqrt(HEAD_DIM))
            q_parts.append(jnp.where(low_half, q, 0.0).astype(jnp.bfloat16))
            q_parts.append(jnp.where(low_half, 0.0, q).astype(jnp.bfloat16))
        scores = lax.dot_general(jnp.concatenate(q_parts, axis=0), band[:, :KV_WIDTH],
                                 (((1,), (1,)), ((), ())),
                                 preferred_element_type=jnp.float32)
        p_parts, inv_parts = [], []
        for i, h in enumerate(HEAD_ORDER):
            sc = jnp.where(valid, scores[i * BLOCK:(i + 1) * BLOCK], 0.0) + logit_ref[table, h]
            e = jnp.exp2(sc - jnp.max(sc, axis=-1, keepdims=True))
            inv_parts.append(1.0 / jnp.sum(e, axis=-1, keepdims=True))
            p_parts.append(jnp.where(valid, e, 0.0).astype(jnp.bfloat16))
        pv = jnp.dot(jnp.concatenate(p_parts, axis=0), band[:, KV_WIDTH:],
                     preferred_element_type=jnp.float32)
        for j in range(len(PAIR_HEADS)):
            lo = pv[2 * j * BLOCK:(2 * j + 1) * BLOCK] * inv_parts[2 * j]
            hi = pv[(2 * j + 1) * BLOCK:(2 * j + 2) * BLOCK] * inv_parts[2 * j + 1]
            col = POOL_WIDTH + j * 2 * HEAD_DIM
            mix_ref[rows, col:col + 2 * HEAD_DIM] = jnp.where(low_half, lo, hi)
        return cur

    prev = kvprev_ref[slot]
    for blk in range(ts // BLOCK):
        prev = attn_block(blk, prev)
    kvprev_ref[1 - slot] = prev

    mix = jnp.dot(mix_ref[...].astype(jnp.bfloat16), w_out_ref[...], preferred_element_type=jnp.float32)
    o_ref[...] = _layer_norm(alpha * x_ref[...] + mix, g_ref[...], b_ref[...])


def _ffn_kernel(h_ref, p_ref, w1_ref, w2_ref, g_ref, b_ref, wple_ref, wgate_ref, bgate_ref, o_ref, *,
                alpha):
    h = h_ref[...]
    hb = h.astype(jnp.bfloat16)
    acc = jnp.zeros(h.shape, jnp.float32)
    for c in range(D_FF // FF_CHUNK):
        cols = slice(c * FF_CHUNK, (c + 1) * FF_CHUNK)
        hid = jnp.maximum(jnp.dot(hb, w1_ref[:, cols], preferred_element_type=jnp.float32), 0.0)
        acc = acc + jnp.dot((hid * hid).astype(jnp.bfloat16), w2_ref[cols, :],
                            preferred_element_type=jnp.float32)
    h2 = _layer_norm(alpha * h + acc, g_ref[...], b_ref[...])
    gate = jax.nn.sigmoid(
        jnp.dot(h2.astype(jnp.bfloat16), wgate_ref[...], preferred_element_type=jnp.float32)
        + bgate_ref[...])
    ple = jnp.dot(p_ref[...].astype(jnp.bfloat16), wple_ref[...], preferred_element_type=jnp.float32)
    o_ref[...] = h2 + gate * ple


def _slab_spec(shape, axis, n_steps, tiles):
    assert shape[axis] % n_steps == 0 and (shape[axis] // n_steps) % BF16_TILE[axis] == 0
    block = tuple(dim // n_steps if a == axis else dim for a, dim in enumerate(shape))
    return pl.BlockSpec(block, lambda b, i: tuple(b * tiles + i if a == axis else 0 for a in range(2)))


def _mixer_call(x, sinks, w_in, w_pool, pool_scale, w_out, ln_g, ln_b, ffn_weights, alpha):
    batch, seq, d = x.shape
    assert d == D_MODEL and seq % MIX_TILE == 0 and MIX_TILE % BLOCK == 0
    tiles = seq // MIX_TILE
    slab_specs = [_slab_spec(w.shape, axis, batch * tiles, tiles) for w, axis in zip(ffn_weights, (1, 0, 0, 0))]
    grid_spec = pl.GridSpec(
        grid=(batch, tiles),
        in_specs=[
            pl.BlockSpec(memory_space=pltpu.SMEM),
            pl.BlockSpec((None, MIX_TILE, d), lambda b, i: (b, i, 0)),
            _const_spec((d, IN_WIDTH)),
            _const_spec((len(POOL_WINDOWS), POOL_GROUP, POOL_GROUP)),
            _const_spec((1, POOL_WIDTH)),
            _const_spec((POOL_WIDTH + ATTN_WIDTH, d)),
            _const_spec((1, d)),
            _const_spec((1, d)),
            *slab_specs,
        ],
        out_specs=[pl.BlockSpec((None, MIX_TILE, d), lambda b, i: (b, i, 0)), *slab_specs],
        scratch_shapes=[
            pltpu.VMEM((MIX_TILE, IN_WIDTH), jnp.float32),
            pltpu.VMEM((MIX_TILE, POOL_WIDTH + ATTN_WIDTH), jnp.float32),
            pltpu.VMEM((2, POOL_HALO, POOL_WIDTH), jnp.float32),
            pltpu.VMEM((2, BLOCK, 2 * KV_WIDTH), jnp.bfloat16),
            pltpu.VMEM((2, N_Q_HEADS, BLOCK, 2 * BLOCK), jnp.float32),
        ],
    )
    return pl.pallas_call(
        functools.partial(_mixer_kernel, alpha=alpha),
        grid_spec=grid_spec,
        out_shape=[jax.ShapeDtypeStruct(x.shape, jnp.float32),
                   *(jax.ShapeDtypeStruct(w.shape, jnp.bfloat16) for w in ffn_weights)],
        compiler_params=pltpu.CompilerParams(
            dimension_semantics=("arbitrary", "arbitrary"), vmem_limit_bytes=VMEM_LIMIT_BYTES),
        name="mixer",
    )(sinks, x, w_in, w_pool, pool_scale, w_out, ln_g, ln_b, *ffn_weights)


def _ffn_call(h, p, w1, w2, ln_g, ln_b, w_ple, w_gate, b_gate, alpha):
    batch, seq, d = h.shape
    assert seq % FFN_TILE == 0
    return pl.pallas_call(
        functools.partial(_ffn_kernel, alpha=alpha),
        grid=(batch, seq // FFN_TILE),
        in_specs=[
            pl.BlockSpec((None, FFN_TILE, d), lambda b, i: (b, i, 0)),
            pl.BlockSpec((None, FFN_TILE, PLE_DIM), lambda b, i: (b, i, 0)),
            _const_spec((d, D_FF)),
            _const_spec((D_FF, d)),
            _const_spec((1, d)),
            _const_spec((1, d)),
            _const_spec((PLE_DIM, d)),
            _const_spec((d, d)),
            _const_spec((1, d)),
        ],
        out_specs=pl.BlockSpec((None, FFN_TILE, d), lambda b, i: (b, i, 0)),
        out_shape=jax.ShapeDtypeStruct(h.shape, jnp.float32),
        compiler_params=pltpu.CompilerParams(
            dimension_semantics=("arbitrary", "arbitrary"), vmem_limit_bytes=VMEM_LIMIT_BYTES),
        name="ffn",
    )(h, p, w1, w2, ln_g, ln_b, w_ple, w_gate, b_gate)


def _pair_heads(w, axis):
    shape = w.shape
    w = w.reshape(shape[:axis] + (N_KV_HEADS, GQA_GROUP, HEAD_DIM) + shape[axis + 1:])
    return jnp.swapaxes(w, axis, axis + 1).reshape(shape)


def kernel(x, p, w_in, w_pool, pool_scale, attn_sinks, w_out, ln1_g, ln1_b, w_ff1, w_ff2, ln2_g, ln2_b,
           w_ple, w_ple_gate, b_ple_gate):
    depth = w_in.shape[0]
    alpha = (2 * depth) ** 0.25
    bf = jnp.bfloat16
    row = lambda v: v.reshape(1, -1)
    q_cols = slice(POOL_WIDTH, POOL_WIDTH + ATTN_WIDTH)
    h = x
    for i in range(depth):
        w_in_i = jnp.concatenate(
            [w_in[i][:, :POOL_WIDTH], _pair_heads(w_in[i][:, q_cols], 1), w_in[i][:, q_cols.stop:]], axis=1)
        w_out_i = jnp.concatenate([w_out[i][:POOL_WIDTH], _pair_heads(w_out[i][q_cols], 0)], axis=0)
        h, w1, w2, wple, wgate = _mixer_call(
            h, attn_sinks[i], w_in_i.astype(bf), w_pool[i], row(pool_scale[i]), w_out_i.astype(bf),
            row(ln1_g[i]), row(ln1_b[i]), (w_ff1[i], w_ff2[i], w_ple[i], w_ple_gate[i]), alpha)
        h = _ffn_call(h, p[i], w1, w2, row(ln2_g[i]), row(ln2_b[i]), wple, wgate, row(b_ple_gate[i]), alpha)
    return h
```

```python
import functools
import math

import jax
import jax.numpy as jnp
from jax import lax
from jax.experimental import pallas as pl
from jax.experimental.pallas import tpu as pltpu

D_MODEL = 1024
PLE_DIM = 256
POOL_WIDTH = 512
POOL_WINDOWS = (2, 4, 8, 16)
POOL_GROUP = 128
HEAD_DIM = 64
ATTN_WIDTH = 512
N_Q_HEADS = 8
N_KV_HEADS = 2
GQA_GROUP = N_Q_HEADS // N_KV_HEADS
KV_WIDTH = N_KV_HEADS * HEAD_DIM
IN_WIDTH = POOL_WIDTH + ATTN_WIDTH + 2 * KV_WIDTH
BLOCK = 128
D_FF = 4 * D_MODEL
LN_EPS = 1e-5
NEG_INF = -1e30
LOG2_E = math.log2(math.e)

POOL_HALO = 16
assert all(w & (w - 1) == 0 and w <= POOL_HALO for w in POOL_WINDOWS)
MIX_TILE = 1024
FFN_TILE = 1024
FF_CHUNK = 512
VMEM_LIMIT_BYTES = 56 * 1024 * 1024
BF16_TILE = (16, 128)

assert N_KV_HEADS == 2 and 2 * HEAD_DIM == 128
PAIR_HEADS = tuple((j, GQA_GROUP + j) for j in range(GQA_GROUP))
HEAD_ORDER = tuple(h for pair in PAIR_HEADS for h in pair)


def _layer_norm(y, g, b):
    mu = jnp.mean(y, axis=-1, keepdims=True)
    yc = y - mu
    var = jnp.mean(yc * yc, axis=-1, keepdims=True)
    return yc * lax.rsqrt(var + LN_EPS) * g + b


def _const_spec(shape):
    return pl.BlockSpec(shape, lambda *_: (0,) * len(shape), pipeline_mode=pl.Buffered(1))


def _mixer_kernel(sinks_ref, x_ref, w_in_ref, w_pool_ref, pool_scale_ref, w_out_pool_ref, w_out_attn_ref,
                  g_ref, b_ref, wf1_ref, wf2_ref, wple_ref, wgate_ref,
                  o_ref, wf1_bf_ref, wf2_bf_ref, wple_bf_ref, wgate_bf_ref,
                  z_ref, mix_ref, halo_ref, kvprev_ref, logit_ref, w_out_ref, *, alpha):
    tile = pl.program_id(1)
    ts = x_ref.shape[0]

    for src, dst in ((wf1_ref, wf1_bf_ref), (wf2_ref, wf2_bf_ref), (wple_ref, wple_bf_ref),
                     (wgate_ref, wgate_bf_ref)):
        dst[...] = src[...].astype(jnp.bfloat16)

    slot = tile % 2

    @pl.when(tile == 0)
    def _():
        halo_ref[0] = jnp.zeros(halo_ref.shape[1:], halo_ref.dtype)
        kvprev_ref[0] = jnp.zeros(kvprev_ref.shape[1:], kvprev_ref.dtype)

    qi = lax.broadcasted_iota(jnp.int32, (BLOCK, 2 * BLOCK), 0)
    kj = lax.broadcasted_iota(jnp.int32, (BLOCK, 2 * BLOCK), 1)
    dist = qi + BLOCK - kj
    in_band = (dist >= 0) & (dist < BLOCK)
    low_half = lax.broadcasted_iota(jnp.int32, (BLOCK, 2 * HEAD_DIM), 1) < HEAD_DIM

    @pl.when((pl.program_id(0) == 0) & (tile == 0))
    def _():
        dist_f = dist.astype(jnp.float32)
        for first in (0, 1):
            ok = in_band & (kj >= BLOCK) if first == 0 else in_band
            for h in range(N_Q_HEADS):
                slope = 2.0 ** (-8.0 * (h + 1) / N_Q_HEADS)
                masked = jnp.where(kj == qi, sinks_ref[h] * LOG2_E, NEG_INF)
                logit_ref[first, h] = jnp.where(ok, (-slope * LOG2_E) * dist_f, masked)
        for g in range(len(POOL_WINDOWS)):
            cols = slice(g * POOL_GROUP, (g + 1) * POOL_GROUP)
            w_out_ref[cols, :] = jnp.dot(
                w_pool_ref[g] * pool_scale_ref[:, cols], w_out_pool_ref[cols, :],
                precision=lax.Precision.HIGHEST, preferred_element_type=jnp.float32).astype(jnp.bfloat16)
        w_out_ref[POOL_WIDTH:, :] = w_out_attn_ref[...]

    xb = x_ref[...].astype(jnp.bfloat16)
    wide = POOL_WIDTH + ATTN_WIDTH
    z_ref[:, :wide] = jnp.dot(xb, w_in_ref[:, :wide], preferred_element_type=jnp.float32)
    for r0 in (0, ts // 2):
        z_ref[r0:r0 + ts // 2, wide:] = jnp.dot(xb[r0:r0 + ts // 2], w_in_ref[:, wide:],
                                                preferred_element_type=jnp.float32)

    halo = halo_ref[slot]
    for g, w in enumerate(POOL_WINDOWS):
        cols = slice(g * POOL_GROUP, (g + 1) * POOL_GROUP)
        u = z_ref[:, cols]
        s = jnp.concatenate([halo[:, cols], u], axis=0)
        shift = 1
        while shift < w:
            s = s + pltpu.roll(s, shift, axis=0)
            shift *= 2
        pos = tile * ts + lax.broadcasted_iota(jnp.int32, (POOL_HALO, POOL_GROUP), 0)
        count = jnp.minimum(pos + 1, w).astype(jnp.float32)
        mean = jnp.concatenate(
            [s[POOL_HALO:2 * POOL_HALO] / count, s[2 * POOL_HALO:] * (1.0 / w)], axis=0)
        mix_ref[:, cols] = mean - u
    halo_ref[1 - slot] = z_ref[ts - POOL_HALO:, :POOL_WIDTH]

    def attn_block(blk, prev):
        rows = slice(blk * BLOCK, (blk + 1) * BLOCK)
        table = jnp.minimum(tile * (ts // BLOCK) + blk, 1)
        valid = in_band & (kj >= BLOCK * (1 - table))

        cur = z_ref[rows, POOL_WIDTH + ATTN_WIDTH:].astype(jnp.bfloat16)
        band = jnp.concatenate([prev, cur], axis=0)
        q_parts = []
        for j, pair in enumerate(PAIR_HEADS):
            col = POOL_WIDTH + j * 2 * HEAD_DIM
            q = z_ref[rows, col:col + 2 * HEAD_DIM] * (LOG2_E / math.sqrt(HEAD_DIM))
            q_parts.append(jnp.where(low_half, q, 0.0).astype(jnp.bfloat16))
            q_parts.append(jnp.where(low_half, 0.0, q).astype(jnp.bfloat16))
        scores = lax.dot_general(jnp.concatenate(q_parts, axis=0), band[:, :KV_WIDTH],
                                 (((1,), (1,)), ((), ())),
                                 preferred_element_type=jnp.float32)
        p_parts, inv_parts = [], []
        for i, h in enumerate(HEAD_ORDER):
            sc = jnp.where(valid, scores[i * BLOCK:(i + 1) * BLOCK], 0.0) + logit_ref[table, h]
            e = jnp.exp2(sc - jnp.max(sc, axis=-1, keepdims=True))
            inv_parts.append(1.0 / jnp.sum(e, axis=-1, keepdims=True))
            p_parts.append(jnp.where(valid, e, 0.0).astype(jnp.bfloat16))
        pv = jnp.dot(jnp.concatenate(p_parts, axis=0), band[:, KV_WIDTH:],
                     preferred_element_type=jnp.float32)
        for j in range(len(PAIR_HEADS)):
            lo = pv[2 * j * BLOCK:(2 * j + 1) * BLOCK] * inv_parts[2 * j]
            hi = pv[(2 * j + 1) * BLOCK:(2 * j + 2) * BLOCK] * inv_parts[2 * j + 1]
            col = POOL_WIDTH + j * 2 * HEAD_DIM
            mix_ref[rows, col:col + 2 * HEAD_DIM] = jnp.where(low_half, lo, hi)
        return cur

    prev = kvprev_ref[slot]
    for blk in range(ts // BLOCK):
        prev = attn_block(blk, prev)
    kvprev_ref[1 - slot] = prev

    mix = jnp.dot(mix_ref[...].astype(jnp.bfloat16), w_out_ref[...], preferred_element_type=jnp.float32)
    o_ref[...] = _layer_norm(alpha * x_ref[...] + mix, g_ref[...], b_ref[...])


def _ffn_kernel(h_ref, p_ref, w1_ref, w2_ref, g_ref, b_ref, wple_ref, wgate_ref, bgate_ref, o_ref, *,
                alpha):
    h = h_ref[...]
    hb = h.astype(jnp.bfloat16)
    acc = jnp.zeros(h.shape, jnp.float32)
    for c in range(D_FF // FF_CHUNK):
        cols = slice(c * FF_CHUNK, (c + 1) * FF_CHUNK)
        hid = jnp.maximum(jnp.dot(hb, w1_ref[:, cols], preferred_element_type=jnp.float32), 0.0)
        acc = acc + jnp.dot((hid * hid).astype(jnp.bfloat16), w2_ref[cols, :],
                            preferred_element_type=jnp.float32)
    h2 = _layer_norm(alpha * h + acc, g_ref[...], b_ref[...])
    gate = jax.nn.sigmoid(
        jnp.dot(h2.astype(jnp.bfloat16), wgate_ref[...], preferred_element_type=jnp.float32)
        + bgate_ref[...])
    ple = jnp.dot(p_ref[...].astype(jnp.bfloat16), wple_ref[...], preferred_element_type=jnp.float32)
    o_ref[...] = h2 + gate * ple


def _slab_spec(shape, axis, n_steps, tiles):
    assert shape[axis] % n_steps == 0 and (shape[axis] // n_steps) % BF16_TILE[axis] == 0
    block = tuple(dim // n_steps if a == axis else dim for a, dim in enumerate(shape))
    return pl.BlockSpec(block, lambda b, i: tuple(b * tiles + i if a == axis else 0 for a in range(2)))


def _mixer_call(x, sinks, w_in, w_pool, pool_scale, w_out, w_out_attn, ln_g, ln_b, ffn_weights, alpha):
    batch, seq, d = x.shape
    assert d == D_MODEL and seq % MIX_TILE == 0 and MIX_TILE % BLOCK == 0
    tiles = seq // MIX_TILE
    slab_specs = [_slab_spec(w.shape, axis, batch * tiles, tiles) for w, axis in zip(ffn_weights, (1, 0, 0, 0))]
    grid_spec = pl.GridSpec(
        grid=(batch, tiles),
        in_specs=[
            pl.BlockSpec(memory_space=pltpu.SMEM),
            pl.BlockSpec((None, MIX_TILE, d), lambda b, i: (b, i, 0)),
            _const_spec((d, IN_WIDTH)),
            _const_spec((len(POOL_WINDOWS), POOL_GROUP, POOL_GROUP)),
            _const_spec((1, POOL_WIDTH)),
            _const_spec((POOL_WIDTH, d)),
            _const_spec((ATTN_WIDTH, d)),
            _const_spec((1, d)),
            _const_spec((1, d)),
            *slab_specs,
        ],
        out_specs=[pl.BlockSpec((None, MIX_TILE, d), lambda b, i: (b, i, 0)), *slab_specs],
        scratch_shapes=[
            pltpu.VMEM((MIX_TILE, IN_WIDTH), jnp.float32),
            pltpu.VMEM((MIX_TILE, POOL_WIDTH + ATTN_WIDTH), jnp.float32),
            pltpu.VMEM((2, POOL_HALO, POOL_WIDTH), jnp.float32),
            pltpu.VMEM((2, BLOCK, 2 * KV_WIDTH), jnp.bfloat16),
            pltpu.VMEM((2, N_Q_HEADS, BLOCK, 2 * BLOCK), jnp.float32),
            pltpu.VMEM((POOL_WIDTH + ATTN_WIDTH, d), jnp.bfloat16),
        ],
    )
    return pl.pallas_call(
        functools.partial(_mixer_kernel, alpha=alpha),
        grid_spec=grid_spec,
        out_shape=[jax.ShapeDtypeStruct(x.shape, jnp.float32),
                   *(jax.ShapeDtypeStruct(w.shape, jnp.bfloat16) for w in ffn_weights)],
        compiler_params=pltpu.CompilerParams(
            dimension_semantics=("arbitrary", "arbitrary"), vmem_limit_bytes=VMEM_LIMIT_BYTES),
        name="mixer",
    )(sinks, x, w_in, w_pool, pool_scale, w_out, w_out_attn, ln_g, ln_b, *ffn_weights)


def _ffn_call(h, p, w1, w2, ln_g, ln_b, w_ple, w_gate, b_gate, alpha):
    batch, seq, d = h.shape
    assert seq % FFN_TILE == 0
    return pl.pallas_call(
        functools.partial(_ffn_kernel, alpha=alpha),
        grid=(batch, seq // FFN_TILE),
        in_specs=[
            pl.BlockSpec((None, FFN_TILE, d), lambda b, i: (b, i, 0)),
            pl.BlockSpec((None, FFN_TILE, PLE_DIM), lambda b, i: (b, i, 0)),
            _const_spec((d, D_FF)),
            _const_spec((D_FF, d)),
            _const_spec((1, d)),
            _const_spec((1, d)),
            _const_spec((PLE_DIM, d)),
            _const_spec((d, d)),
            _const_spec((1, d)),
        ],
        out_specs=pl.BlockSpec((None, FFN_TILE, d), lambda b, i: (b, i, 0)),
        out_shape=jax.ShapeDtypeStruct(h.shape, jnp.float32),
        compiler_params=pltpu.CompilerParams(
            dimension_semantics=("arbitrary", "arbitrary"), vmem_limit_bytes=VMEM_LIMIT_BYTES),
        name="ffn",
    )(h, p, w1, w2, ln_g, ln_b, w_ple, w_gate, b_gate)


def _pair_heads(w, axis):
    shape = w.shape
    w = w.reshape(shape[:axis] + (N_KV_HEADS, GQA_GROUP, HEAD_DIM) + shape[axis + 1:])
    return jnp.swapaxes(w, axis, axis + 1).reshape(shape)


def kernel(x, p, w_in, w_pool, pool_scale, attn_sinks, w_out, ln1_g, ln1_b, w_ff1, w_ff2, ln2_g, ln2_b,
           w_ple, w_ple_gate, b_ple_gate):
    depth = w_in.shape[0]
    alpha = (2 * depth) ** 0.25
    bf = jnp.bfloat16
    row = lambda v: v.reshape(1, -1)
    q_cols = slice(POOL_WIDTH, POOL_WIDTH + ATTN_WIDTH)
    h = x
    for i in range(depth):
        w_in_i = jnp.concatenate(
            [w_in[i][:, :POOL_WIDTH], _pair_heads(w_in[i][:, q_cols], 1), w_in[i][:, q_cols.stop:]], axis=1)
        w_out_attn = _pair_heads(w_out[i][q_cols], 0).astype(bf)
        h, w1, w2, wple, wgate = _mixer_call(
            h, attn_sinks[i], w_in_i.astype(bf), w_pool[i], row(pool_scale[i]), w_out[i], w_out_attn,
            row(ln1_g[i]), row(ln1_b[i]), (w_ff1[i], w_ff2[i], w_ple[i], w_ple_gate[i]), alpha)
        h = _ffn_call(h, p[i], w1, w2, row(ln2_g[i]), row(ln2_b[i]), wple, wgate, row(b_ple_gate[i]), alpha)
    return h
```

```python
import functools
import math

import jax
import jax.numpy as jnp
from jax import lax
from jax.experimental import pallas as pl
from jax.experimental.pallas import tpu as pltpu

D_MODEL = 1024
PLE_DIM = 256
POOL_WIDTH = 512
POOL_WINDOWS = (2, 4, 8, 16)
POOL_GROUP = 128
HEAD_DIM = 64
ATTN_WIDTH = 512
N_Q_HEADS = 8
N_KV_HEADS = 2
GQA_GROUP = N_Q_HEADS // N_KV_HEADS
KV_WIDTH = N_KV_HEADS * HEAD_DIM
IN_WIDTH = POOL_WIDTH + ATTN_WIDTH + 2 * KV_WIDTH
BLOCK = 128
D_FF = 4 * D_MODEL
LN_EPS = 1e-5
NEG_INF = -1e30
LOG2_E = math.log2(math.e)

POOL_HALO = 16
assert all(w & (w - 1) == 0 and w <= POOL_HALO for w in POOL_WINDOWS)
MIX_TILE = 1024
FFN_TILE = 1024
FF_CHUNK = 512
VMEM_LIMIT_BYTES = 56 * 1024 * 1024
BF16_TILE = (16, 128)

assert N_KV_HEADS == 2 and 2 * HEAD_DIM == 128 and GQA_GROUP % 2 == 0
PAIR_HEADS = tuple((j, GQA_GROUP + j) for j in range(GQA_GROUP))
HEAD_ORDER = tuple(h for pair in PAIR_HEADS for h in pair)


def _layer_norm(y, g, b):
    mu = jnp.mean(y, axis=-1, keepdims=True)
    yc = y - mu
    var = jnp.mean(yc * yc, axis=-1, keepdims=True)
    return yc * lax.rsqrt(var + LN_EPS) * g + b


def _const_spec(shape):
    return pl.BlockSpec(shape, lambda *_: (0,) * len(shape), pipeline_mode=pl.Buffered(1))


def _mixer_kernel(sinks_ref, x_ref, w_in_f32_ref, w_pool_ref, pool_scale_ref, w_out_f32_ref,
                  g_ref, b_ref, wf1_ref, wf2_ref, wple_ref, wgate_ref,
                  o_ref, wf1_bf_ref, wf2_bf_ref, wple_bf_ref, wgate_bf_ref,
                  z_ref, mix_ref, halo_ref, kvprev_ref, logit_ref, w_in_ref, w_out_ref, *, alpha):
    tile = pl.program_id(1)
    ts = x_ref.shape[0]

    for src, dst in ((wf1_ref, wf1_bf_ref), (wf2_ref, wf2_bf_ref), (wple_ref, wple_bf_ref),
                     (wgate_ref, wgate_bf_ref)):
        dst[...] = src[...].astype(jnp.bfloat16)

    slot = tile % 2

    @pl.when(tile == 0)
    def _():
        halo_ref[0] = jnp.zeros(halo_ref.shape[1:], halo_ref.dtype)
        kvprev_ref[0] = jnp.zeros(kvprev_ref.shape[1:], kvprev_ref.dtype)

    qi = lax.broadcasted_iota(jnp.int32, (BLOCK, 2 * BLOCK), 0)
    kj = lax.broadcasted_iota(jnp.int32, (BLOCK, 2 * BLOCK), 1)
    dist = qi + BLOCK - kj
    in_band = (dist >= 0) & (dist < BLOCK)
    low_half = lax.broadcasted_iota(jnp.int32, (BLOCK, 2 * HEAD_DIM), 1) < HEAD_DIM

    @pl.when((pl.program_id(0) == 0) & (tile == 0))
    def _():
        dist_f = dist.astype(jnp.float32)
        for first in (0, 1):
            ok = in_band & (kj >= BLOCK) if first == 0 else in_band
            for h in range(N_Q_HEADS):
                slope = 2.0 ** (-8.0 * (h + 1) / N_Q_HEADS)
                masked = jnp.where(kj == qi, sinks_ref[h] * LOG2_E, NEG_INF * LOG2_E)
                logit_ref[first, h] = jnp.where(ok, (-slope * LOG2_E) * dist_f, masked)
        q_end = POOL_WIDTH + ATTN_WIDTH
        w_in_ref[:, :POOL_WIDTH] = w_in_f32_ref[:, :POOL_WIDTH].astype(jnp.bfloat16)
        w_in_ref[:, q_end:] = w_in_f32_ref[:, q_end:].astype(jnp.bfloat16)
        low_w = lax.broadcasted_iota(jnp.int32, (w_in_ref.shape[0], 2 * HEAD_DIM), 1) < HEAD_DIM
        for j, (h0, h1) in enumerate(PAIR_HEADS):
            src0, src1 = (POOL_WIDTH + (h // 2) * 2 * HEAD_DIM for h in (h0, h1))
            a = w_in_f32_ref[:, src0:src0 + 2 * HEAD_DIM]
            b = w_in_f32_ref[:, src1:src1 + 2 * HEAD_DIM]
            if h0 % 2 == 0:
                pair = jnp.where(low_w, a, pltpu.roll(b, HEAD_DIM, axis=1))
            else:
                pair = jnp.where(low_w, pltpu.roll(a, HEAD_DIM, axis=1), b)
            col = POOL_WIDTH + j * 2 * HEAD_DIM
            w_in_ref[:, col:col + 2 * HEAD_DIM] = pair.astype(jnp.bfloat16)
        for g in range(len(POOL_WINDOWS)):
            cols = slice(g * POOL_GROUP, (g + 1) * POOL_GROUP)
            w_out_ref[cols, :] = jnp.dot(
                w_pool_ref[g] * pool_scale_ref[:, cols], w_out_f32_ref[cols, :],
                precision=lax.Precision.HIGHEST, preferred_element_type=jnp.float32).astype(jnp.bfloat16)
        for i, h in enumerate(HEAD_ORDER):
            src, dst = POOL_WIDTH + h * HEAD_DIM, POOL_WIDTH + i * HEAD_DIM
            w_out_ref[dst:dst + HEAD_DIM, :] = w_out_f32_ref[src:src + HEAD_DIM, :].astype(jnp.bfloat16)

    xb = x_ref[...].astype(jnp.bfloat16)
    wide = POOL_WIDTH + ATTN_WIDTH
    z_ref[:, :wide] = jnp.dot(xb, w_in_ref[:, :wide], preferred_element_type=jnp.float32)
    for r0 in (0, ts // 2):
        z_ref[r0:r0 + ts // 2, wide:] = jnp.dot(xb[r0:r0 + ts // 2], w_in_ref[:, wide:],
                                                preferred_element_type=jnp.float32)

    halo = halo_ref[slot]
    for g, w in enumerate(POOL_WINDOWS):
        cols = slice(g * POOL_GROUP, (g + 1) * POOL_GROUP)
        u = z_ref[:, cols]
        s = jnp.concatenate([halo[:, cols], u], axis=0)
        shift = 1
        while shift < w:
            s = s + pltpu.roll(s, shift, axis=0)
            shift *= 2
        pos = tile * ts + lax.broadcasted_iota(jnp.int32, (POOL_HALO, POOL_GROUP), 0)
        count = jnp.minimum(pos + 1, w).astype(jnp.float32)
        mean = jnp.concatenate(
            [s[POOL_HALO:2 * POOL_HALO] / count, s[2 * POOL_HALO:] * (1.0 / w)], axis=0)
        mix_ref[:, cols] = mean - u
    halo_ref[1 - slot] = z_ref[ts - POOL_HALO:, :POOL_WIDTH]

    def attn_block(blk, prev):
        rows = slice(blk * BLOCK, (blk + 1) * BLOCK)
        table = jnp.minimum(tile * (ts // BLOCK) + blk, 1)
        valid = in_band & (kj >= BLOCK * (1 - table))

        cur = z_ref[rows, POOL_WIDTH + ATTN_WIDTH:].astype(jnp.bfloat16)
        band = jnp.concatenate([prev, cur], axis=0)
        q_parts = []
        for j, pair in enumerate(PAIR_HEADS):
            col = POOL_WIDTH + j * 2 * HEAD_DIM
            q = z_ref[rows, col:col + 2 * HEAD_DIM] * (LOG2_E / math.sqrt(HEAD_DIM))
            q_parts.append(jnp.where(low_half, q, 0.0).astype(jnp.bfloat16))
            q_parts.append(jnp.where(low_half, 0.0, q).astype(jnp.bfloat16))
        scores = lax.dot_general(jnp.concatenate(q_parts, axis=0), band[:, :KV_WIDTH],
                                 (((1,), (1,)), ((), ())),
                                 preferred_element_type=jnp.float32)
        p_parts, inv_parts = [], []
        for i, h in enumerate(HEAD_ORDER):
            sc = jnp.where(valid, scores[i * BLOCK:(i + 1) * BLOCK], 0.0) + logit_ref[table, h]
            e = jnp.exp2(sc - jnp.max(sc, axis=-1, keepdims=True))
            inv_parts.append(1.0 / jnp.sum(e, axis=-1, keepdims=True))
            p_parts.append(jnp.where(valid, e, 0.0).astype(jnp.bfloat16))
        pv = jnp.dot(jnp.concatenate(p_parts, axis=0), band[:, KV_WIDTH:],
                     preferred_element_type=jnp.float32)
        for j in range(len(PAIR_HEADS)):
            lo = pv[2 * j * BLOCK:(2 * j + 1) * BLOCK] * inv_parts[2 * j]
            hi = pv[(2 * j + 1) * BLOCK:(2 * j + 2) * BLOCK] * inv_parts[2 * j + 1]
            col = POOL_WIDTH + j * 2 * HEAD_DIM
            mix_ref[rows, col:col + 2 * HEAD_DIM] = jnp.where(low_half, lo, hi)
        return cur

    prev = kvprev_ref[slot]
    for blk in range(ts // BLOCK):
        prev = attn_block(blk, prev)
    kvprev_ref[1 - slot] = prev

    mix = jnp.dot(mix_ref[...].astype(jnp.bfloat16), w_out_ref[...], preferred_element_type=jnp.float32)
    o_ref[...] = _layer_norm(alpha * x_ref[...] + mix, g_ref[...], b_ref[...])


def _ffn_kernel(h_ref, p_ref, w1_ref, w2_ref, g_ref, b_ref, wple_ref, wgate_ref, bgate_ref, o_ref, *,
                alpha):
    h = h_ref[...]
    hb = h.astype(jnp.bfloat16)
    acc = jnp.zeros(h.shape, jnp.float32)
    for c in range(D_FF // FF_CHUNK):
        cols = slice(c * FF_CHUNK, (c + 1) * FF_CHUNK)
        hid = jnp.maximum(jnp.dot(hb, w1_ref[:, cols], preferred_element_type=jnp.float32), 0.0)
        acc = acc + jnp.dot((hid * hid).astype(jnp.bfloat16), w2_ref[cols, :],
                            preferred_element_type=jnp.float32)
    h2 = _layer_norm(alpha * h + acc, g_ref[...], b_ref[...])
    gate = jax.nn.sigmoid(
        jnp.dot(h2.astype(jnp.bfloat16), wgate_ref[...], preferred_element_type=jnp.float32)
        + bgate_ref[...])
    ple = jnp.dot(p_ref[...].astype(jnp.bfloat16), wple_ref[...], preferred_element_type=jnp.float32)
    o_ref[...] = h2 + gate * ple


def _slab_spec(shape, axis, n_steps, tiles):
    assert shape[axis] % n_steps == 0 and (shape[axis] // n_steps) % BF16_TILE[axis] == 0
    block = tuple(dim // n_steps if a == axis else dim for a, dim in enumerate(shape))
    return pl.BlockSpec(block, lambda b, i: tuple(b * tiles + i if a == axis else 0 for a in range(2)))


def _mixer_call(x, sinks, w_in, w_pool, pool_scale, w_out, ln_g, ln_b, ffn_weights, alpha):
    batch, seq, d = x.shape
    assert d == D_MODEL and seq % MIX_TILE == 0 and MIX_TILE % BLOCK == 0
    tiles = seq // MIX_TILE
    slab_specs = [_slab_spec(w.shape, axis, batch * tiles, tiles) for w, axis in zip(ffn_weights, (1, 0, 0, 0))]
    grid_spec = pl.GridSpec(
        grid=(batch, tiles),
        in_specs=[
            pl.BlockSpec(memory_space=pltpu.SMEM),
            pl.BlockSpec((None, MIX_TILE, d), lambda b, i: (b, i, 0)),
            _const_spec((d, IN_WIDTH)),
            _const_spec((len(POOL_WINDOWS), POOL_GROUP, POOL_GROUP)),
            _const_spec((1, POOL_WIDTH)),
            _const_spec((POOL_WIDTH + ATTN_WIDTH, d)),
            _const_spec((1, d)),
            _const_spec((1, d)),
            *slab_specs,
        ],
        out_specs=[pl.BlockSpec((None, MIX_TILE, d), lambda b, i: (b, i, 0)), *slab_specs],
        scratch_shapes=[
            pltpu.VMEM((MIX_TILE, IN_WIDTH), jnp.float32),
            pltpu.VMEM((MIX_TILE, POOL_WIDTH + ATTN_WIDTH), jnp.float32),
            pltpu.VMEM((2, POOL_HALO, POOL_WIDTH), jnp.float32),
            pltpu.VMEM((2, BLOCK, 2 * KV_WIDTH), jnp.bfloat16),
            pltpu.VMEM((2, N_Q_HEADS, BLOCK, 2 * BLOCK), jnp.float32),
            pltpu.VMEM((d, IN_WIDTH), jnp.bfloat16),
            pltpu.VMEM((POOL_WIDTH + ATTN_WIDTH, d), jnp.bfloat16),
        ],
    )
    return pl.pallas_call(
        functools.partial(_mixer_kernel, alpha=alpha),
        grid_spec=grid_spec,
        out_shape=[jax.ShapeDtypeStruct(x.shape, jnp.float32),
                   *(jax.ShapeDtypeStruct(w.shape, jnp.bfloat16) for w in ffn_weights)],
        compiler_params=pltpu.CompilerParams(
            dimension_semantics=("arbitrary", "arbitrary"), vmem_limit_bytes=VMEM_LIMIT_BYTES),
        name="mixer",
    )(sinks, x, w_in, w_pool, pool_scale, w_out, ln_g, ln_b, *ffn_weights)


def _ffn_call(h, p, w1, w2, ln_g, ln_b, w_ple, w_gate, b_gate, alpha):
    batch, seq, d = h.shape
    assert seq % FFN_TILE == 0
    return pl.pallas_call(
        functools.partial(_ffn_kernel, alpha=alpha),
        grid=(batch, seq // FFN_TILE),
        in_specs=[
            pl.BlockSpec((None, FFN_TILE, d), lambda b, i: (b, i, 0)),
            pl.BlockSpec((None, FFN_TILE, PLE_DIM), lambda b, i: (b, i, 0)),
            _const_spec((d, D_FF)),
            _const_spec((D_FF, d)),
            _const_spec((1, d)),
            _const_spec((1, d)),
            _const_spec((PLE_DIM, d)),
            _const_spec((d, d)),
            _const_spec((1, d)),
        ],
        out_specs=pl.BlockSpec((None, FFN_TILE, d), lambda b, i: (b, i, 0)),
        out_shape=jax.ShapeDtypeStruct(h.shape, jnp.float32),
        compiler_params=pltpu.CompilerParams(
            dimension_semantics=("arbitrary", "arbitrary"), vmem_limit_bytes=VMEM_LIMIT_BYTES),
        name="ffn",
    )(h, p, w1, w2, ln_g, ln_b, w_ple, w_gate, b_gate)


def kernel(x, p, w_in, w_pool, pool_scale, attn_sinks, w_out, ln1_g, ln1_b, w_ff1, w_ff2, ln2_g, ln2_b,
           w_ple, w_ple_gate, b_ple_gate):
    depth = w_in.shape[0]
    alpha = (2 * depth) ** 0.25
    row = lambda v: v.reshape(1, -1)
    h = x
    for i in range(depth):
        h, w1, w2, wple, wgate = _mixer_call(
            h, attn_sinks[i], w_in[i], w_pool[i], row(pool_scale[i]), w_out[i],
            row(ln1_g[i]), row(ln1_b[i]), (w_ff1[i], w_ff2[i], w_ple[i], w_ple_gate[i]), alpha)
        h = _ffn_call(h, p[i], w1, w2, row(ln2_g[i]), row(ln2_b[i]), wple, wgate, row(b_ple_gate[i]), alpha)
    return h
```

```python
import functools
import math

import jax
import jax.numpy as jnp
from jax import lax
from jax.experimental import pallas as pl
from jax.experimental.pallas import tpu as pltpu

D_MODEL = 1024
PLE_DIM = 256
POOL_WIDTH = 512
POOL_WINDOWS = (2, 4, 8, 16)
POOL_GROUP = 128
HEAD_DIM = 64
ATTN_WIDTH = 512
N_Q_HEADS = 8
N_KV_HEADS = 2
GQA_GROUP = N_Q_HEADS // N_KV_HEADS
KV_WIDTH = N_KV_HEADS * HEAD_DIM
IN_WIDTH = POOL_WIDTH + ATTN_WIDTH + 2 * KV_WIDTH
BLOCK = 128
D_FF = 4 * D_MODEL
LN_EPS = 1e-5
NEG_INF = -1e30
LOG2_E = math.log2(math.e)

POOL_HALO = 16
assert all(w & (w - 1) == 0 and w <= POOL_HALO for w in POOL_WINDOWS)
MIX_TILE = 1024
FFN_TILE = 1024
FF_CHUNK = 512
VMEM_LIMIT_BYTES = 56 * 1024 * 1024
BF16_TILE = (16, 128)

assert N_KV_HEADS == 2 and 2 * HEAD_DIM == 128 and GQA_GROUP % 2 == 0
PAIR_HEADS = tuple((j, GQA_GROUP + j) for j in range(GQA_GROUP))
HEAD_ORDER = tuple(h for pair in PAIR_HEADS for h in pair)


def _layer_norm_of_scaled(v, g, b, alpha):
    mu = jnp.mean(v, axis=-1, keepdims=True)
    vc = v - mu
    var = jnp.mean(vc * vc, axis=-1, keepdims=True)
    return vc * lax.rsqrt(var + LN_EPS / (alpha * alpha)) * g + b


def _const_spec(shape):
    return pl.BlockSpec(shape, lambda *_: (0,) * len(shape), pipeline_mode=pl.Buffered(1))


def _mixer_kernel(sinks_ref, x_ref, w_in_f32_ref, w_pool_ref, pool_scale_ref, w_out_f32_ref,
                  g_ref, b_ref, wf1_ref, wf2_ref, wple_ref, wgate_ref,
                  o_ref, wf1_bf_ref, wf2_bf_ref, wple_bf_ref, wgate_bf_ref,
                  z_ref, mix_ref, halo_ref, kvprev_ref, logit_ref, w_in_ref, w_out_ref, *, alpha):
    tile = pl.program_id(1)
    ts = x_ref.shape[0]

    for src, dst in ((wf1_ref, wf1_bf_ref), (wple_ref, wple_bf_ref), (wgate_ref, wgate_bf_ref)):
        dst[...] = src[...].astype(jnp.bfloat16)
    wf2_bf_ref[...] = (wf2_ref[...] * (1.0 / alpha)).astype(jnp.bfloat16)

    slot = tile % 2

    @pl.when(tile == 0)
    def _():
        halo_ref[0] = jnp.zeros(halo_ref.shape[1:], halo_ref.dtype)
        kvprev_ref[0] = jnp.zeros(kvprev_ref.shape[1:], kvprev_ref.dtype)

    qi = lax.broadcasted_iota(jnp.int32, (BLOCK, 2 * BLOCK), 0)
    kj = lax.broadcasted_iota(jnp.int32, (BLOCK, 2 * BLOCK), 1)
    dist = qi + BLOCK - kj
    in_band = (dist >= 0) & (dist < BLOCK)
    low_half = lax.broadcasted_iota(jnp.int32, (BLOCK, 2 * HEAD_DIM), 1) < HEAD_DIM

    @pl.when((pl.program_id(0) == 0) & (tile == 0))
    def _():
        dist_f = dist.astype(jnp.float32)
        for first in (0, 1):
            ok = in_band & (kj >= BLOCK) if first == 0 else in_band
            for h in range(N_Q_HEADS):
                slope = 2.0 ** (-8.0 * (h + 1) / N_Q_HEADS)
                masked = jnp.where(kj == qi, sinks_ref[h] * LOG2_E, NEG_INF * LOG2_E)
                logit_ref[first, h] = jnp.where(ok, (-slope * LOG2_E) * dist_f, masked)
        q_end = POOL_WIDTH + ATTN_WIDTH
        w_in_ref[:, :POOL_WIDTH] = w_in_f32_ref[:, :POOL_WIDTH].astype(jnp.bfloat16)
        w_in_ref[:, q_end:] = w_in_f32_ref[:, q_end:].astype(jnp.bfloat16)
        low_w = lax.broadcasted_iota(jnp.int32, (w_in_ref.shape[0], 2 * HEAD_DIM), 1) < HEAD_DIM
        for j, (h0, h1) in enumerate(PAIR_HEADS):
            src0, src1 = (POOL_WIDTH + (h // 2) * 2 * HEAD_DIM for h in (h0, h1))
            a = w_in_f32_ref[:, src0:src0 + 2 * HEAD_DIM]
            b = w_in_f32_ref[:, src1:src1 + 2 * HEAD_DIM]
            if h0 % 2 == 0:
                pair = jnp.where(low_w, a, pltpu.roll(b, HEAD_DIM, axis=1))
            else:
                pair = jnp.where(low_w, pltpu.roll(a, HEAD_DIM, axis=1), b)
            col = POOL_WIDTH + j * 2 * HEAD_DIM
            w_in_ref[:, col:col + 2 * HEAD_DIM] = pair.astype(jnp.bfloat16)
        for g in range(len(POOL_WINDOWS)):
            cols = slice(g * POOL_GROUP, (g + 1) * POOL_GROUP)
            w_out_ref[cols, :] = (jnp.dot(
                w_pool_ref[g] * pool_scale_ref[:, cols], w_out_f32_ref[cols, :],
                precision=lax.Precision.HIGHEST, preferred_element_type=jnp.float32)
                                  * (1.0 / alpha)).astype(jnp.bfloat16)
        for i, h in enumerate(HEAD_ORDER):
            src, dst = POOL_WIDTH + h * HEAD_DIM, POOL_WIDTH + i * HEAD_DIM
            w_out_ref[dst:dst + HEAD_DIM, :] = (w_out_f32_ref[src:src + HEAD_DIM, :]
                                                * (1.0 / alpha)).astype(jnp.bfloat16)

    xb = x_ref[...].astype(jnp.bfloat16)
    half = ts // 2
    halo = halo_ref[slot]
    for r0 in (0, half):
        rows = slice(r0, r0 + half)
        z_ref[rows, :] = jnp.dot(xb[rows], w_in_ref[...], preferred_element_type=jnp.float32)

        for g, w in enumerate(POOL_WINDOWS):
            cols = slice(g * POOL_GROUP, (g + 1) * POOL_GROUP)
            u = z_ref[rows, cols]
            s = jnp.concatenate([halo[:, cols], u], axis=0)
            shift = 1
            while shift < w:
                s = s + pltpu.roll(s, shift, axis=0)
                shift *= 2
            mean = s[POOL_HALO:] * (1.0 / w)
            if r0 == 0:
                pos = tile * ts + lax.broadcasted_iota(jnp.int32, (POOL_HALO, POOL_GROUP), 0)
                count = jnp.minimum(pos + 1, w).astype(jnp.float32)
                mean = jnp.concatenate([s[POOL_HALO:2 * POOL_HALO] / count, mean[POOL_HALO:]], axis=0)
            mix_ref[rows, cols] = mean - u
        halo = z_ref[r0 + half - POOL_HALO:r0 + half, :POOL_WIDTH]
    halo_ref[1 - slot] = halo

    def attn_block(blk, prev):
        rows = slice(blk * BLOCK, (blk + 1) * BLOCK)
        table = jnp.minimum(tile * (ts // BLOCK) + blk, 1)
        valid = in_band & (kj >= BLOCK * (1 - table))

        cur = z_ref[rows, POOL_WIDTH + ATTN_WIDTH:].astype(jnp.bfloat16)
        band = jnp.concatenate([prev, cur], axis=0)
        q_parts = []
        for j, pair in enumerate(PAIR_HEADS):
            col = POOL_WIDTH + j * 2 * HEAD_DIM
            q = z_ref[rows, col:col + 2 * HEAD_DIM] * (LOG2_E / math.sqrt(HEAD_DIM))
            q_parts.append(jnp.where(low_half, q, 0.0).astype(jnp.bfloat16))
            q_parts.append(jnp.where(low_half, 0.0, q).astype(jnp.bfloat16))
        scores = lax.dot_general(jnp.concatenate(q_parts, axis=0), band[:, :KV_WIDTH],
                                 (((1,), (1,)), ((), ())),
                                 preferred_element_type=jnp.float32)
        p_parts, inv_parts = [], []
        for i, h in enumerate(HEAD_ORDER):
            sc = jnp.where(valid, scores[i * BLOCK:(i + 1) * BLOCK], 0.0) + logit_ref[table, h]
            e = jnp.exp2(sc - jnp.max(sc, axis=-1, keepdims=True))
            inv_parts.append(1.0 / jnp.sum(e, axis=-1, keepdims=True))
            p_parts.append(jnp.where(valid, e, 0.0).astype(jnp.bfloat16))
        pv = jnp.dot(jnp.concatenate(p_parts, axis=0), band[:, KV_WIDTH:],
                     preferred_element_type=jnp.float32)
        for j in range(len(PAIR_HEADS)):
            lo = pv[2 * j * BLOCK:(2 * j + 1) * BLOCK] * inv_parts[2 * j]
            hi = pv[(2 * j + 1) * BLOCK:(2 * j + 2) * BLOCK] * inv_parts[2 * j + 1]
            col = POOL_WIDTH + j * 2 * HEAD_DIM
            mix_ref[rows, col:col + 2 * HEAD_DIM] = jnp.where(low_half, lo, hi)
        return cur

    prev = kvprev_ref[slot]
    for blk in range(ts // BLOCK):
        prev = attn_block(blk, prev)
    kvprev_ref[1 - slot] = prev

    mix = jnp.dot(mix_ref[...].astype(jnp.bfloat16), w_out_ref[...], preferred_element_type=jnp.float32)
    o_ref[...] = _layer_norm_of_scaled(x_ref[...] + mix, g_ref[...], b_ref[...], alpha)


def _ffn_kernel(h_ref, p_ref, w1_ref, w2_ref, g_ref, b_ref, wple_ref, wgate_ref, bgate_ref, o_ref, *,
                alpha):
    h = h_ref[...]
    hb = h.astype(jnp.bfloat16)
    acc = jnp.zeros(h.shape, jnp.float32)
    for c in range(D_FF // FF_CHUNK):
        cols = slice(c * FF_CHUNK, (c + 1) * FF_CHUNK)
        hid = jnp.maximum(jnp.dot(hb, w1_ref[:, cols], preferred_element_type=jnp.float32), 0.0)
        acc = acc + jnp.dot((hid * hid).astype(jnp.bfloat16), w2_ref[cols, :],
                            preferred_element_type=jnp.float32)
    h2 = _layer_norm_of_scaled(h + acc, g_ref[...], b_ref[...], alpha)
    gate = jax.nn.sigmoid(
        jnp.dot(h2.astype(jnp.bfloat16), wgate_ref[...], preferred_element_type=jnp.float32)
        + bgate_ref[...])
    ple = jnp.dot(p_ref[...].astype(jnp.bfloat16), wple_ref[...], preferred_element_type=jnp.float32)
    o_ref[...] = h2 + gate * ple


def _slab_spec(shape, axis, n_steps, tiles):
    assert shape[axis] % n_steps == 0 and (shape[axis] // n_steps) % BF16_TILE[axis] == 0
    block = tuple(dim // n_steps if a == axis else dim for a, dim in enumerate(shape))
    return pl.BlockSpec(block, lambda b, i: tuple(b * tiles + i if a == axis else 0 for a in range(2)))


def _mixer_call(x, sinks, w_in, w_pool, pool_scale, w_out, ln_g, ln_b, ffn_weights, alpha):
    batch, seq, d = x.shape
    assert d == D_MODEL and seq % MIX_TILE == 0 and MIX_TILE % BLOCK == 0
    tiles = seq // MIX_TILE
    slab_specs = [_slab_spec(w.shape, axis, batch * tiles, tiles) for w, axis in zip(ffn_weights, (1, 0, 0, 0))]
    grid_spec = pl.GridSpec(
        grid=(batch, tiles),
        in_specs=[
            pl.BlockSpec(memory_space=pltpu.SMEM),
            pl.BlockSpec((None, MIX_TILE, d), lambda b, i: (b, i, 0)),
            _const_spec((d, IN_WIDTH)),
            _const_spec((len(POOL_WINDOWS), POOL_GROUP, POOL_GROUP)),
            _const_spec((1, POOL_WIDTH)),
            _const_spec((POOL_WIDTH + ATTN_WIDTH, d)),
            _const_spec((1, d)),
            _const_spec((1, d)),
            *slab_specs,
        ],
        out_specs=[pl.BlockSpec((None, MIX_TILE, d), lambda b, i: (b, i, 0)), *slab_specs],
        scratch_shapes=[
            pltpu.VMEM((MIX_TILE, IN_WIDTH), jnp.float32),
            pltpu.VMEM((MIX_TILE, POOL_WIDTH + ATTN_WIDTH), jnp.float32),
            pltpu.VMEM((2, POOL_HALO, POOL_WIDTH), jnp.float32),
            pltpu.VMEM((2, BLOCK, 2 * KV_WIDTH), jnp.bfloat16),
            pltpu.VMEM((2, N_Q_HEADS, BLOCK, 2 * BLOCK), jnp.float32),
            pltpu.VMEM((d, IN_WIDTH), jnp.bfloat16),
            pltpu.VMEM((POOL_WIDTH + ATTN_WIDTH, d), jnp.bfloat16),
        ],
    )
    return pl.pallas_call(
        functools.partial(_mixer_kernel, alpha=alpha),
        grid_spec=grid_spec,
        out_shape=[jax.ShapeDtypeStruct(x.shape, jnp.float32),
                   *(jax.ShapeDtypeStruct(w.shape, jnp.bfloat16) for w in ffn_weights)],
        compiler_params=pltpu.CompilerParams(
            dimension_semantics=("arbitrary", "arbitrary"), vmem_limit_bytes=VMEM_LIMIT_BYTES),
        name="mixer",
    )(sinks, x, w_in, w_pool, pool_scale, w_out, ln_g, ln_b, *ffn_weights)


def _ffn_call(h, p, w1, w2, ln_g, ln_b, w_ple, w_gate, b_gate, alpha):
    batch, seq, d = h.shape
    assert seq % FFN_TILE == 0
    return pl.pallas_call(
        functools.partial(_ffn_kernel, alpha=alpha),
        grid=(batch, seq // FFN_TILE),
        in_specs=[
            pl.BlockSpec((None, FFN_TILE, d), lambda b, i: (b, i, 0)),
            pl.BlockSpec((None, FFN_TILE, PLE_DIM), lambda b, i: (b, i, 0)),
            _const_spec((d, D_FF)),
            _const_spec((D_FF, d)),
            _const_spec((1, d)),
            _const_spec((1, d)),
            _const_spec((PLE_DIM, d)),
            _const_spec((d, d)),
            _const_spec((1, d)),
        ],
        out_specs=pl.BlockSpec((None, FFN_TILE, d), lambda b, i: (b, i, 0)),
        out_shape=jax.ShapeDtypeStruct(h.shape, jnp.float32),
        compiler_params=pltpu.CompilerParams(
            dimension_semantics=("arbitrary", "arbitrary"), vmem_limit_bytes=VMEM_LIMIT_BYTES),
        name="ffn",
    )(h, p, w1, w2, ln_g, ln_b, w_ple, w_gate, b_gate)


def kernel(x, p, w_in, w_pool, pool_scale, attn_sinks, w_out, ln1_g, ln1_b, w_ff1, w_ff2, ln2_g, ln2_b,
           w_ple, w_ple_gate, b_ple_gate):
    depth = w_in.shape[0]
    alpha = (2 * depth) ** 0.25
    row = lambda v: v.reshape(1, -1)
    h = x
    for i in range(depth):
        h, w1, w2, wple, wgate = _mixer_call(
            h, attn_sinks[i], w_in[i], w_pool[i], row(pool_scale[i]), w_out[i],
            row(ln1_g[i]), row(ln1_b[i]), (w_ff1[i], w_ff2[i], w_ple[i], w_ple_gate[i]), alpha)
        h = _ffn_call(h, p[i], w1, w2, row(ln2_g[i]), row(ln2_b[i]), wple, wgate, row(b_ple_gate[i]), alpha)
    return h
```

```python
import functools
import math

import jax
import jax.numpy as jnp
from jax import lax
from jax.experimental import pallas as pl
from jax.experimental.pallas import tpu as pltpu

D_MODEL = 1024
PLE_DIM = 256
POOL_WIDTH = 512
POOL_WINDOWS = (2, 4, 8, 16)
POOL_GROUP = 128
HEAD_DIM = 64
ATTN_WIDTH = 512
N_Q_HEADS = 8
N_KV_HEADS = 2
GQA_GROUP = N_Q_HEADS // N_KV_HEADS
KV_WIDTH = N_KV_HEADS * HEAD_DIM
IN_WIDTH = POOL_WIDTH + ATTN_WIDTH + 2 * KV_WIDTH
BLOCK = 128
D_FF = 4 * D_MODEL
LN_EPS = 1e-5
NEG_INF = -1e30
LOG2_E = math.log2(math.e)

POOL_HALO = 16
assert all(w & (w - 1) == 0 and w <= POOL_HALO for w in POOL_WINDOWS)
MIX_TILE = 1024
FFN_TILE = 1024
FF_CHUNK = 512
VMEM_LIMIT_BYTES = 56 * 1024 * 1024
BF16_TILE = (16, 128)

assert N_KV_HEADS == 2 and 2 * HEAD_DIM == 128 and GQA_GROUP % 2 == 0
PAIR_HEADS = tuple((j, GQA_GROUP + j) for j in range(GQA_GROUP))
HEAD_ORDER = tuple(h for pair in PAIR_HEADS for h in pair)


def _layer_norm_of_scaled(v, g, b, alpha):
    mu = jnp.mean(v, axis=-1, keepdims=True)
    vc = v - mu
    var = jnp.mean(vc * vc, axis=-1, keepdims=True)
    return vc * lax.rsqrt(var + LN_EPS / (alpha * alpha)) * g + b


def _const_spec(shape):
    return pl.BlockSpec(shape, lambda *_: (0,) * len(shape), pipeline_mode=pl.Buffered(1))


def _mixer_kernel(sinks_ref, x_ref, w_in_f32_ref, w_pool_ref, pool_scale_ref, w_out_f32_ref,
                  g_ref, b_ref, wf1_ref, wf2_ref, wple_ref, wgate_ref,
                  o_ref, wf1_bf_ref, wf2_bf_ref, wple_bf_ref, wgate_bf_ref,
                  z_ref, mix_ref, halo_ref, kvprev_ref, logit_ref, w_in_ref, w_out_ref, *, alpha):
    tile = pl.program_id(1)
    ts = x_ref.shape[0]

    for src, dst in ((wf1_ref, wf1_bf_ref), (wple_ref, wple_bf_ref), (wgate_ref, wgate_bf_ref)):
        dst[...] = src[...].astype(jnp.bfloat16)
    wf2_bf_ref[...] = (wf2_ref[...] * (1.0 / alpha)).astype(jnp.bfloat16)

    slot = tile % 2

    @pl.when(tile == 0)
    def _():
        halo_ref[0] = jnp.zeros(halo_ref.shape[1:], halo_ref.dtype)
        kvprev_ref[0] = jnp.zeros(kvprev_ref.shape[1:], kvprev_ref.dtype)

    qi = lax.broadcasted_iota(jnp.int32, (BLOCK, 2 * BLOCK), 0)
    kj = lax.broadcasted_iota(jnp.int32, (BLOCK, 2 * BLOCK), 1)
    dist = qi + BLOCK - kj
    in_band = (dist >= 0) & (dist < BLOCK)
    low_half = lax.broadcasted_iota(jnp.int32, (BLOCK, 2 * HEAD_DIM), 1) < HEAD_DIM

    @pl.when((pl.program_id(0) == 0) & (tile == 0))
    def _():
        dist_f = dist.astype(jnp.float32)
        for first in (0, 1):
            ok = in_band & (kj >= BLOCK) if first == 0 else in_band
            for h in range(N_Q_HEADS):
                slope = 2.0 ** (-8.0 * (h + 1) / N_Q_HEADS)
                masked = jnp.where(kj == qi, sinks_ref[h] * LOG2_E, NEG_INF * LOG2_E)
                logit_ref[first, h] = jnp.where(ok, (-slope * LOG2_E) * dist_f, masked)
        q_end = POOL_WIDTH + ATTN_WIDTH
        w_in_ref[:, :POOL_WIDTH] = w_in_f32_ref[:, :POOL_WIDTH].astype(jnp.bfloat16)
        w_in_ref[:, q_end:] = w_in_f32_ref[:, q_end:].astype(jnp.bfloat16)
        low_w = lax.broadcasted_iota(jnp.int32, (w_in_ref.shape[0], 2 * HEAD_DIM), 1) < HEAD_DIM
        for j, (h0, h1) in enumerate(PAIR_HEADS):
            src0, src1 = (POOL_WIDTH + (h // 2) * 2 * HEAD_DIM for h in (h0, h1))
            a = w_in_f32_ref[:, src0:src0 + 2 * HEAD_DIM]
            b = w_in_f32_ref[:, src1:src1 + 2 * HEAD_DIM]
            if h0 % 2 == 0:
                pair = jnp.where(low_w, a, pltpu.roll(b, HEAD_DIM, axis=1))
            else:
                pair = jnp.where(low_w, pltpu.roll(a, HEAD_DIM, axis=1), b)
            col = POOL_WIDTH + j * 2 * HEAD_DIM
            w_in_ref[:, col:col + 2 * HEAD_DIM] = pair.astype(jnp.bfloat16)
        for g in range(len(POOL_WINDOWS)):
            cols = slice(g * POOL_GROUP, (g + 1) * POOL_GROUP)
            w_out_ref[cols, :] = (jnp.dot(
                w_pool_ref[g] * pool_scale_ref[:, cols], w_out_f32_ref[cols, :],
                precision=lax.Precision.HIGHEST, preferred_element_type=jnp.float32)
                                  * (1.0 / alpha)).astype(jnp.bfloat16)
        for i, h in enumerate(HEAD_ORDER):
            src, dst = POOL_WIDTH + h * HEAD_DIM, POOL_WIDTH + i * HEAD_DIM
            w_out_ref[dst:dst + HEAD_DIM, :] = (w_out_f32_ref[src:src + HEAD_DIM, :]
                                                * (1.0 / alpha)).astype(jnp.bfloat16)

    xb = x_ref[...].astype(jnp.bfloat16)
    half = ts // 2
    halo = halo_ref[slot]
    for r0 in (0, half):
        rows = slice(r0, r0 + half)
        z_ref[rows, :] = jnp.dot(xb[rows], w_in_ref[...], preferred_element_type=jnp.float32)

        for g, w in enumerate(POOL_WINDOWS):
            cols = slice(g * POOL_GROUP, (g + 1) * POOL_GROUP)
            u = z_ref[rows, cols]
            s = jnp.concatenate([halo[:, cols], u], axis=0)
            shift = 1
            while shift < w:
                s = s + pltpu.roll(s, shift, axis=0)
                shift *= 2
            mean = s[POOL_HALO:] * (1.0 / w)
            if r0 == 0:
                pos = tile * ts + lax.broadcasted_iota(jnp.int32, (POOL_HALO, POOL_GROUP), 0)
                count = jnp.minimum(pos + 1, w).astype(jnp.float32)
                mean = jnp.concatenate([s[POOL_HALO:2 * POOL_HALO] / count, mean[POOL_HALO:]], axis=0)
            mix_ref[rows, cols] = mean - u
        halo = z_ref[r0 + half - POOL_HALO:r0 + half, :POOL_WIDTH]
    halo_ref[1 - slot] = halo

    def attn_block(blk, prev):
        rows = slice(blk * BLOCK, (blk + 1) * BLOCK)
        table = jnp.minimum(tile * (ts // BLOCK) + blk, 1)
        valid = in_band & (kj >= BLOCK * (1 - table))

        cur = z_ref[rows, POOL_WIDTH + ATTN_WIDTH:].astype(jnp.bfloat16)
        band = jnp.concatenate([prev, cur], axis=0)
        q_parts = []
        for j, pair in enumerate(PAIR_HEADS):
            col = POOL_WIDTH + j * 2 * HEAD_DIM
            q = z_ref[rows, col:col + 2 * HEAD_DIM] * (LOG2_E / math.sqrt(HEAD_DIM))
            q_parts.append(jnp.where(low_half, q, 0.0).astype(jnp.bfloat16))
            q_parts.append(jnp.where(low_half, 0.0, q).astype(jnp.bfloat16))
        scores = lax.dot_general(jnp.concatenate(q_parts, axis=0), band[:, :KV_WIDTH],
                                 (((1,), (1,)), ((), ())),
                                 preferred_element_type=jnp.float32)
        p_parts, inv_parts = [], []
        for i, h in enumerate(HEAD_ORDER):
            sc = jnp.where(valid, scores[i * BLOCK:(i + 1) * BLOCK], 0.0) + logit_ref[table, h]
            e = jnp.exp2(sc - jnp.max(sc, axis=-1, keepdims=True))
            inv_parts.append(1.0 / jnp.sum(e, axis=-1, keepdims=True))
            p_parts.append(jnp.where(valid, e, 0.0).astype(jnp.bfloat16))
        pv = jnp.dot(jnp.concatenate(p_parts, axis=0), band[:, KV_WIDTH:],
                     preferred_element_type=jnp.float32)
        for j in range(len(PAIR_HEADS)):
            lo = pv[2 * j * BLOCK:(2 * j + 1) * BLOCK] * inv_parts[2 * j]
            hi = pv[(2 * j + 1) * BLOCK:(2 * j + 2) * BLOCK] * inv_parts[2 * j + 1]
            col = POOL_WIDTH + j * 2 * HEAD_DIM
            mix_ref[rows, col:col + 2 * HEAD_DIM] = jnp.where(low_half, lo, hi)
        return cur

    prev = kvprev_ref[slot]
    for blk in range(ts // BLOCK):
        prev = attn_block(blk, prev)
    kvprev_ref[1 - slot] = prev

    mix = jnp.dot(mix_ref[...].astype(jnp.bfloat16), w_out_ref[...], preferred_element_type=jnp.float32)
    o_ref[...] = _layer_norm_of_scaled(x_ref[...] + mix, g_ref[...], b_ref[...], alpha)


def _ffn_kernel(h_ref, p_ref, w1_ref, w2_ref, g_ref, b_ref, wple_ref, wgate_ref, bgate_ref, o_ref, *,
                alpha):
    h = h_ref[...]
    hb = h.astype(jnp.bfloat16)
    acc = h
    for c in range(D_FF // FF_CHUNK):
        cols = slice(c * FF_CHUNK, (c + 1) * FF_CHUNK)
        hid = jnp.maximum(jnp.dot(hb, w1_ref[:, cols], preferred_element_type=jnp.float32), 0.0)
        acc = acc + jnp.dot((hid * hid).astype(jnp.bfloat16), w2_ref[cols, :],
                            preferred_element_type=jnp.float32)
    h2 = _layer_norm_of_scaled(acc, g_ref[...], b_ref[...], alpha)
    gate = jax.nn.sigmoid(
        jnp.dot(h2.astype(jnp.bfloat16), wgate_ref[...], preferred_element_type=jnp.float32)
        + bgate_ref[...])
    ple = jnp.dot(p_ref[...].astype(jnp.bfloat16), wple_ref[...], preferred_element_type=jnp.float32)
    o_ref[...] = h2 + gate * ple


def _slab_spec(shape, axis, n_steps, tiles):
    assert shape[axis] % n_steps == 0 and (shape[axis] // n_steps) % BF16_TILE[axis] == 0
    block = tuple(dim // n_steps if a == axis else dim for a, dim in enumerate(shape))
    return pl.BlockSpec(block, lambda b, i: tuple(b * tiles + i if a == axis else 0 for a in range(2)))


def _mixer_call(x, sinks, w_in, w_pool, pool_scale, w_out, ln_g, ln_b, ffn_weights, alpha):
    batch, seq, d = x.shape
    assert d == D_MODEL and seq % MIX_TILE == 0 and MIX_TILE % BLOCK == 0
    tiles = seq // MIX_TILE
    slab_specs = [_slab_spec(w.shape, axis, batch * tiles, tiles) for w, axis in zip(ffn_weights, (1, 0, 0, 0))]
    grid_spec = pl.GridSpec(
        grid=(batch, tiles),
        in_specs=[
            pl.BlockSpec(memory_space=pltpu.SMEM),
            pl.BlockSpec((None, MIX_TILE, d), lambda b, i: (b, i, 0)),
            _const_spec((d, IN_WIDTH)),
            _const_spec((len(POOL_WINDOWS), POOL_GROUP, POOL_GROUP)),
            _const_spec((1, POOL_WIDTH)),
            _const_spec((POOL_WIDTH + ATTN_WIDTH, d)),
            _const_spec((1, d)),
            _const_spec((1, d)),
            *slab_specs,
        ],
        out_specs=[pl.BlockSpec((None, MIX_TILE, d), lambda b, i: (b, i, 0)), *slab_specs],
        scratch_shapes=[
            pltpu.VMEM((MIX_TILE, IN_WIDTH), jnp.float32),
            pltpu.VMEM((MIX_TILE, POOL_WIDTH + ATTN_WIDTH), jnp.float32),
            pltpu.VMEM((2, POOL_HALO, POOL_WIDTH), jnp.float32),
            pltpu.VMEM((2, BLOCK, 2 * KV_WIDTH), jnp.bfloat16),
            pltpu.VMEM((2, N_Q_HEADS, BLOCK, 2 * BLOCK), jnp.float32),
            pltpu.VMEM((d, IN_WIDTH), jnp.bfloat16),
            pltpu.VMEM((POOL_WIDTH + ATTN_WIDTH, d), jnp.bfloat16),
        ],
    )
    return pl.pallas_call(
        functools.partial(_mixer_kernel, alpha=alpha),
        grid_spec=grid_spec,
        out_shape=[jax.ShapeDtypeStruct(x.shape, jnp.float32),
                   *(jax.ShapeDtypeStruct(w.shape, jnp.bfloat16) for w in ffn_weights)],
        compiler_params=pltpu.CompilerParams(
            dimension_semantics=("arbitrary", "arbitrary"), vmem_limit_bytes=VMEM_LIMIT_BYTES),
        name="mixer",
    )(sinks, x, w_in, w_pool, pool_scale, w_out, ln_g, ln_b, *ffn_weights)


def _ffn_call(h, p, w1, w2, ln_g, ln_b, w_ple, w_gate, b_gate, alpha):
    batch, seq, d = h.shape
    assert seq % FFN_TILE == 0
    return pl.pallas_call(
        functools.partial(_ffn_kernel, alpha=alpha),
        grid=(batch, seq // FFN_TILE),
        in_specs=[
            pl.BlockSpec((None, FFN_TILE, d), lambda b, i: (b, i, 0)),
            pl.BlockSpec((None, FFN_TILE, PLE_DIM), lambda b, i: (b, i, 0)),
            _const_spec((d, D_FF)),
            _const_spec((D_FF, d)),
            _const_spec((1, d)),
            _const_spec((1, d)),
            _const_spec((PLE_DIM, d)),
            _const_spec((d, d)),
            _const_spec((1, d)),
        ],
        out_specs=pl.BlockSpec((None, FFN_TILE, d), lambda b, i: (b, i, 0)),
        out_shape=jax.ShapeDtypeStruct(h.shape, jnp.float32),
        compiler_params=pltpu.CompilerParams(
            dimension_semantics=("arbitrary", "arbitrary"), vmem_limit_bytes=VMEM_LIMIT_BYTES),
        name="ffn",
    )(h, p, w1, w2, ln_g, ln_b, w_ple, w_gate, b_gate)


def kernel(x, p, w_in, w_pool, pool_scale, attn_sinks, w_out, ln1_g, ln1_b, w_ff1, w_ff2, ln2_g, ln2_b,
           w_ple, w_ple_gate, b_ple_gate):
    depth = w_in.shape[0]
    alpha = (2 * depth) ** 0.25
    row = lambda v: v.reshape(1, -1)
    h = x
    for i in range(depth):
        h, w1, w2, wple, wgate = _mixer_call(
            h, attn_sinks[i], w_in[i], w_pool[i], row(pool_scale[i]), w_out[i],
            row(ln1_g[i]), row(ln1_b[i]), (w_ff1[i], w_ff2[i], w_ple[i], w_ple_gate[i]), alpha)
        h = _ffn_call(h, p[i], w1, w2, row(ln2_g[i]), row(ln2_b[i]), wple, wgate, row(b_ple_gate[i]), alpha)
    return h
```

```python
import functools
import math

import jax
import jax.numpy as jnp
from jax import lax
from jax.experimental import pallas as pl
from jax.experimental.pallas import tpu as pltpu

D_MODEL = 1024
PLE_DIM = 256
POOL_WIDTH = 512
POOL_WINDOWS = (2, 4, 8, 16)
POOL_GROUP = 128
HEAD_DIM = 64
ATTN_WIDTH = 512
N_Q_HEADS = 8
N_KV_HEADS = 2
GQA_GROUP = N_Q_HEADS // N_KV_HEADS
KV_WIDTH = N_KV_HEADS * HEAD_DIM
IN_WIDTH = POOL_WIDTH + ATTN_WIDTH + 2 * KV_WIDTH
BLOCK = 128
D_FF = 4 * D_MODEL
LN_EPS = 1e-5
NEG_INF = -1e30
LOG2_E = math.log2(math.e)

POOL_HALO = 16
assert all(w & (w - 1) == 0 and w <= POOL_HALO for w in POOL_WINDOWS)
MIX_TILE = 1024
FFN_TILE = 1024
FF_CHUNK = 512
VMEM_LIMIT_BYTES = 56 * 1024 * 1024
BF16_TILE = (16, 128)

assert N_KV_HEADS == 2 and 2 * HEAD_DIM == 128 and GQA_GROUP % 2 == 0
PAIR_HEADS = tuple((j, GQA_GROUP + j) for j in range(GQA_GROUP))
HEAD_ORDER = tuple(h for pair in PAIR_HEADS for h in pair)


def _layer_norm_of_scaled(v, g, b, alpha):
    mu = jnp.mean(v, axis=-1, keepdims=True)
    vc = v - mu
    var = jnp.mean(vc * vc, axis=-1, keepdims=True)
    return vc * lax.rsqrt(var + LN_EPS / (alpha * alpha)) * g + b


def _const_spec(shape):
    return pl.BlockSpec(shape, lambda *_: (0,) * len(shape), pipeline_mode=pl.Buffered(1))


def _mixer_kernel(sinks_ref, x_ref, w_in_f32_ref, w_pool_ref, pool_scale_ref, w_out_f32_ref,
                  g_ref, b_ref, wf1_ref, wf2_ref, wple_ref, wgate_ref,
                  o_ref, wf1_bf_ref, wf2_bf_ref, wple_bf_ref, wgate_bf_ref,
                  z_ref, mix_ref, halo_ref, kvprev_ref, logit_ref, w_in_ref, w_out_ref, *, alpha):
    tile = pl.program_id(1)
    ts = x_ref.shape[0]

    for src, dst in ((wf1_ref, wf1_bf_ref), (wple_ref, wple_bf_ref), (wgate_ref, wgate_bf_ref)):
        dst[...] = src[...].astype(jnp.bfloat16)
    wf2_bf_ref[...] = (wf2_ref[...] * (1.0 / alpha)).astype(jnp.bfloat16)

    slot = tile % 2

    @pl.when(tile == 0)
    def _():
        halo_ref[0] = jnp.zeros(halo_ref.shape[1:], halo_ref.dtype)
        kvprev_ref[0] = jnp.zeros(kvprev_ref.shape[1:], kvprev_ref.dtype)

    qi = lax.broadcasted_iota(jnp.int32, (BLOCK, 2 * BLOCK), 0)
    kj = lax.broadcasted_iota(jnp.int32, (BLOCK, 2 * BLOCK), 1)
    dist = qi + BLOCK - kj
    in_band = (dist >= 0) & (dist < BLOCK)
    low_half = lax.broadcasted_iota(jnp.int32, (BLOCK, 2 * HEAD_DIM), 1) < HEAD_DIM

    @pl.when((pl.program_id(0) == 0) & (tile == 0))
    def _():
        dist_f = dist.astype(jnp.float32)
        for first in (0, 1):
            ok = in_band & (kj >= BLOCK) if first == 0 else in_band
            for h in range(N_Q_HEADS):
                slope = 2.0 ** (-8.0 * (h + 1) / N_Q_HEADS)
                masked = jnp.where(kj == qi, sinks_ref[h] * LOG2_E, NEG_INF * LOG2_E)
                logit_ref[first, h] = jnp.where(ok, (-slope * LOG2_E) * dist_f, masked)
        q_end = POOL_WIDTH + ATTN_WIDTH
        w_in_ref[:, :POOL_WIDTH] = w_in_f32_ref[:, :POOL_WIDTH].astype(jnp.bfloat16)
        w_in_ref[:, q_end:] = w_in_f32_ref[:, q_end:].astype(jnp.bfloat16)
        low_w = lax.broadcasted_iota(jnp.int32, (w_in_ref.shape[0], 2 * HEAD_DIM), 1) < HEAD_DIM
        for j, (h0, h1) in enumerate(PAIR_HEADS):
            src0, src1 = (POOL_WIDTH + (h // 2) * 2 * HEAD_DIM for h in (h0, h1))
            a = w_in_f32_ref[:, src0:src0 + 2 * HEAD_DIM]
            b = w_in_f32_ref[:, src1:src1 + 2 * HEAD_DIM]
            if h0 % 2 == 0:
                pair = jnp.where(low_w, a, pltpu.roll(b, HEAD_DIM, axis=1))
            else:
                pair = jnp.where(low_w, pltpu.roll(a, HEAD_DIM, axis=1), b)
            col = POOL_WIDTH + j * 2 * HEAD_DIM
            w_in_ref[:, col:col + 2 * HEAD_DIM] = (pair * (LOG2_E / math.sqrt(HEAD_DIM))).astype(jnp.bfloat16)
        for g in range(len(POOL_WINDOWS)):
            cols = slice(g * POOL_GROUP, (g + 1) * POOL_GROUP)
            w_out_ref[cols, :] = (jnp.dot(
                w_pool_ref[g] * pool_scale_ref[:, cols], w_out_f32_ref[cols, :],
                precision=lax.Precision.HIGHEST, preferred_element_type=jnp.float32)
                                  * (1.0 / alpha)).astype(jnp.bfloat16)
        for i, h in enumerate(HEAD_ORDER):
            src, dst = POOL_WIDTH + h * HEAD_DIM, POOL_WIDTH + i * HEAD_DIM
            w_out_ref[dst:dst + HEAD_DIM, :] = (w_out_f32_ref[src:src + HEAD_DIM, :]
                                                * (1.0 / alpha)).astype(jnp.bfloat16)

    xb = x_ref[...].astype(jnp.bfloat16)
    half = ts // 2
    halo = halo_ref[slot]
    for r0 in (0, half):
        rows = slice(r0, r0 + half)
        z_ref[rows, :] = jnp.dot(xb[rows], w_in_ref[...], preferred_element_type=jnp.float32)

        for g, w in enumerate(POOL_WINDOWS):
            cols = slice(g * POOL_GROUP, (g + 1) * POOL_GROUP)
            u = z_ref[rows, cols]
            s = jnp.concatenate([halo[:, cols], u], axis=0)
            shift = 1
            while shift < w:
                s = s + pltpu.roll(s, shift, axis=0)
                shift *= 2
            mean = s[POOL_HALO:] * (1.0 / w)
            if r0 == 0:
                pos = tile * ts + lax.broadcasted_iota(jnp.int32, (POOL_HALO, POOL_GROUP), 0)
                count = jnp.minimum(pos + 1, w).astype(jnp.float32)
                mean = jnp.concatenate([s[POOL_HALO:2 * POOL_HALO] / count, mean[POOL_HALO:]], axis=0)
            mix_ref[rows, cols] = mean - u
        halo = z_ref[r0 + half - POOL_HALO:r0 + half, :POOL_WIDTH]
    halo_ref[1 - slot] = halo

    def attn_block(blk, prev):
        rows = slice(blk * BLOCK, (blk + 1) * BLOCK)
        table = jnp.minimum(tile * (ts // BLOCK) + blk, 1)
        valid = in_band & (kj >= BLOCK * (1 - table))

        cur = z_ref[rows, POOL_WIDTH + ATTN_WIDTH:].astype(jnp.bfloat16)
        band = jnp.concatenate([prev, cur], axis=0)
        q_parts = []
        for j, pair in enumerate(PAIR_HEADS):
            col = POOL_WIDTH + j * 2 * HEAD_DIM
            q = z_ref[rows, col:col + 2 * HEAD_DIM]
            q_parts.append(jnp.where(low_half, q, 0.0).astype(jnp.bfloat16))
            q_parts.append(jnp.where(low_half, 0.0, q).astype(jnp.bfloat16))
        scores = lax.dot_general(jnp.concatenate(q_parts, axis=0), band[:, :KV_WIDTH],
                                 (((1,), (1,)), ((), ())),
                                 preferred_element_type=jnp.float32)
        p_parts, inv_parts = [], []
        for i, h in enumerate(HEAD_ORDER):
            sc = jnp.where(valid, scores[i * BLOCK:(i + 1) * BLOCK], 0.0) + logit_ref[table, h]
            e = jnp.exp2(sc - jnp.max(sc, axis=-1, keepdims=True))
            inv_parts.append(1.0 / jnp.sum(e, axis=-1, keepdims=True))
            p_parts.append(jnp.where(valid, e, 0.0).astype(jnp.bfloat16))
        pv = jnp.dot(jnp.concatenate(p_parts, axis=0), band[:, KV_WIDTH:],
                     preferred_element_type=jnp.float32)
        for j in range(len(PAIR_HEADS)):
            lo = pv[2 * j * BLOCK:(2 * j + 1) * BLOCK] * inv_parts[2 * j]
            hi = pv[(2 * j + 1) * BLOCK:(2 * j + 2) * BLOCK] * inv_parts[2 * j + 1]
            col = POOL_WIDTH + j * 2 * HEAD_DIM
            mix_ref[rows, col:col + 2 * HEAD_DIM] = jnp.where(low_half, lo, hi)
        return cur

    prev = kvprev_ref[slot]
    for blk in range(ts // BLOCK):
        prev = attn_block(blk, prev)
    kvprev_ref[1 - slot] = prev

    mix = jnp.dot(mix_ref[...].astype(jnp.bfloat16), w_out_ref[...], preferred_element_type=jnp.float32)
    o_ref[...] = _layer_norm_of_scaled(x_ref[...] + mix, g_ref[...], b_ref[...], alpha)


def _ffn_kernel(h_ref, p_ref, w1_ref, w2_ref, g_ref, b_ref, wple_ref, wgate_ref, bgate_ref, o_ref, *,
                alpha):
    h = h_ref[...]
    hb = h.astype(jnp.bfloat16)
    acc = h
    for c in range(D_FF // FF_CHUNK):
        cols = slice(c * FF_CHUNK, (c + 1) * FF_CHUNK)
        hid = jnp.maximum(jnp.dot(hb, w1_ref[:, cols], preferred_element_type=jnp.float32), 0.0)
        acc = acc + jnp.dot((hid * hid).astype(jnp.bfloat16), w2_ref[cols, :],
                            preferred_element_type=jnp.float32)
    h2 = _layer_norm_of_scaled(acc, g_ref[...], b_ref[...], alpha)
    gate = jax.nn.sigmoid(
        jnp.dot(h2.astype(jnp.bfloat16), wgate_ref[...], preferred_element_type=jnp.float32)
        + bgate_ref[...])
    ple = jnp.dot(p_ref[...].astype(jnp.bfloat16), wple_ref[...], preferred_element_type=jnp.float32)
    o_ref[...] = h2 + gate * ple


def _slab_spec(shape, axis, n_steps, tiles):
    assert shape[axis] % n_steps == 0 and (shape[axis] // n_steps) % BF16_TILE[axis] == 0
    block = tuple(dim // n_steps if a == axis else dim for a, dim in enumerate(shape))
    return pl.BlockSpec(block, lambda b, i: tuple(b * tiles + i if a == axis else 0 for a in range(2)))


def _mixer_call(x, sinks, w_in, w_pool, pool_scale, w_out, ln_g, ln_b, ffn_weights, alpha):
    batch, seq, d = x.shape
    assert d == D_MODEL and seq % MIX_TILE == 0 and MIX_TILE % BLOCK == 0
    tiles = seq // MIX_TILE
    slab_specs = [_slab_spec(w.shape, axis, batch * tiles, tiles) for w, axis in zip(ffn_weights, (1, 0, 0, 0))]
    grid_spec = pl.GridSpec(
        grid=(batch, tiles),
        in_specs=[
            pl.BlockSpec(memory_space=pltpu.SMEM),
            pl.BlockSpec((None, MIX_TILE, d), lambda b, i: (b, i, 0)),
            _const_spec((d, IN_WIDTH)),
            _const_spec((len(POOL_WINDOWS), POOL_GROUP, POOL_GROUP)),
            _const_spec((1, POOL_WIDTH)),
            _const_spec((POOL_WIDTH + ATTN_WIDTH, d)),
            _const_spec((1, d)),
            _const_spec((1, d)),
            *slab_specs,
        ],
        out_specs=[pl.BlockSpec((None, MIX_TILE, d), lambda b, i: (b, i, 0)), *slab_specs],
        scratch_shapes=[
            pltpu.VMEM((MIX_TILE, IN_WIDTH), jnp.float32),
            pltpu.VMEM((MIX_TILE, POOL_WIDTH + ATTN_WIDTH), jnp.float32),
            pltpu.VMEM((2, POOL_HALO, POOL_WIDTH), jnp.float32),
            pltpu.VMEM((2, BLOCK, 2 * KV_WIDTH), jnp.bfloat16),
            pltpu.VMEM((2, N_Q_HEADS, BLOCK, 2 * BLOCK), jnp.float32),
            pltpu.VMEM((d, IN_WIDTH), jnp.bfloat16),
            pltpu.VMEM((POOL_WIDTH + ATTN_WIDTH, d), jnp.bfloat16),
        ],
    )
    return pl.pallas_call(
        functools.partial(_mixer_kernel, alpha=alpha),
        grid_spec=grid_spec,
        out_shape=[jax.ShapeDtypeStruct(x.shape, jnp.float32),
                   *(jax.ShapeDtypeStruct(w.shape, jnp.bfloat16) for w in ffn_weights)],
        compiler_params=pltpu.CompilerParams(
            dimension_semantics=("arbitrary", "arbitrary"), vmem_limit_bytes=VMEM_LIMIT_BYTES),
        name="mixer",
    )(sinks, x, w_in, w_pool, pool_scale, w_out, ln_g, ln_b, *ffn_weights)


def _ffn_call(h, p, w1, w2, ln_g, ln_b, w_ple, w_gate, b_gate, alpha):
    batch, seq, d = h.shape
    assert seq % FFN_TILE == 0
    return pl.pallas_call(
        functools.partial(_ffn_kernel, alpha=alpha),
        grid=(batch, seq // FFN_TILE),
        in_specs=[
            pl.BlockSpec((None, FFN_TILE, d), lambda b, i: (b, i, 0)),
            pl.BlockSpec((None, FFN_TILE, PLE_DIM), lambda b, i: (b, i, 0)),
            _const_spec((d, D_FF)),
            _const_spec((D_FF, d)),
            _const_spec((1, d)),
            _const_spec((1, d)),
            _const_spec((PLE_DIM, d)),
            _const_spec((d, d)),
            _const_spec((1, d)),
        ],
        out_specs=pl.BlockSpec((None, FFN_TILE, d), lambda b, i: (b, i, 0)),
        out_shape=jax.ShapeDtypeStruct(h.shape, jnp.float32),
        compiler_params=pltpu.CompilerParams(
            dimension_semantics=("arbitrary", "arbitrary"), vmem_limit_bytes=VMEM_LIMIT_BYTES),
        name="ffn",
    )(h, p, w1, w2, ln_g, ln_b, w_ple, w_gate, b_gate)


def kernel(x, p, w_in, w_pool, pool_scale, attn_sinks, w_out, ln1_g, ln1_b, w_ff1, w_ff2, ln2_g, ln2_b,
           w_ple, w_ple_gate, b_ple_gate):
    depth = w_in.shape[0]
    alpha = (2 * depth) ** 0.25
    row = lambda v: v.reshape(1, -1)
    h = x
    for i in range(depth):
        h, w1, w2, wple, wgate = _mixer_call(
            h, attn_sinks[i], w_in[i], w_pool[i], row(pool_scale[i]), w_out[i],
            row(ln1_g[i]), row(ln1_b[i]), (w_ff1[i], w_ff2[i], w_ple[i], w_ple_gate[i]), alpha)
        h = _ffn_call(h, p[i], w1, w2, row(ln2_g[i]), row(ln2_b[i]), wple, wgate, row(b_ple_gate[i]), alpha)
    return h
```

```python
import functools
import math

import jax
import jax.numpy as jnp
from jax import lax
from jax.experimental import pallas as pl
from jax.experimental.pallas import tpu as pltpu

D_MODEL = 1024
PLE_DIM = 256
POOL_WIDTH = 512
POOL_WINDOWS = (2, 4, 8, 16)
POOL_GROUP = 128
HEAD_DIM = 64
ATTN_WIDTH = 512
N_Q_HEADS = 8
N_KV_HEADS = 2
GQA_GROUP = N_Q_HEADS // N_KV_HEADS
KV_WIDTH = N_KV_HEADS * HEAD_DIM
IN_WIDTH = POOL_WIDTH + ATTN_WIDTH + 2 * KV_WIDTH
BLOCK = 128
D_FF = 4 * D_MODEL
LN_EPS = 1e-5
NEG_INF = -1e30
LOG2_E = math.log2(math.e)

POOL_HALO = 16
assert all(w & (w - 1) == 0 and w <= POOL_HALO for w in POOL_WINDOWS)
MIX_TILE = 1024
FFN_TILE = 1024
FF_CHUNK = 512
VMEM_LIMIT_BYTES = 56 * 1024 * 1024
BF16_TILE = (16, 128)

assert N_KV_HEADS == 2 and 2 * HEAD_DIM == 128 and GQA_GROUP % 2 == 0
PAIR_HEADS = tuple((j, GQA_GROUP + j) for j in range(GQA_GROUP))
HEAD_ORDER = tuple(h for pair in PAIR_HEADS for h in pair)


def _layer_norm_of_scaled(v, g, b, alpha):
    mu = jnp.mean(v, axis=-1, keepdims=True)
    vc = v - mu
    var = jnp.mean(vc * vc, axis=-1, keepdims=True)
    return vc * lax.rsqrt(var + LN_EPS / (alpha * alpha)) * g + b


def _const_spec(shape):
    return pl.BlockSpec(shape, lambda *_: (0,) * len(shape), pipeline_mode=pl.Buffered(1))


def _mixer_kernel(sinks_ref, x_ref, w_in_f32_ref, w_pool_ref, pool_scale_ref, w_out_f32_ref,
                  g_ref, b_ref, wf1_ref, wf2_ref, wple_ref, wgate_ref,
                  o_ref, wf1_bf_ref, wf2_bf_ref, wple_bf_ref, wgate_bf_ref,
                  z_ref, mix_ref, halo_ref, kvprev_ref, logit_ref, w_in_ref, w_out_ref, *, alpha):
    tile = pl.program_id(1)
    ts = x_ref.shape[0]

    for src, dst in ((wf1_ref, wf1_bf_ref), (wple_ref, wple_bf_ref), (wgate_ref, wgate_bf_ref)):
        dst[...] = src[...].astype(jnp.bfloat16)
    wf2_bf_ref[...] = (wf2_ref[...] * (1.0 / alpha)).astype(jnp.bfloat16)

    slot = tile % 2

    @pl.when(tile == 0)
    def _():
        halo_ref[0] = jnp.zeros(halo_ref.shape[1:], halo_ref.dtype)
        kvprev_ref[0] = jnp.zeros(kvprev_ref.shape[1:], kvprev_ref.dtype)

    qi = lax.broadcasted_iota(jnp.int32, (BLOCK, 2 * BLOCK), 0)
    kj = lax.broadcasted_iota(jnp.int32, (BLOCK, 2 * BLOCK), 1)
    dist = qi + BLOCK - kj
    in_band = (dist >= 0) & (dist < BLOCK)
    low_half = lax.broadcasted_iota(jnp.int32, (BLOCK, 2 * HEAD_DIM), 1) < HEAD_DIM

    @pl.when((pl.program_id(0) == 0) & (tile == 0))
    def _():
        dist_f = dist.astype(jnp.float32)
        for first in (0, 1):
            ok = in_band & (kj >= BLOCK) if first == 0 else in_band
            for h in range(N_Q_HEADS):
                slope = 2.0 ** (-8.0 * (h + 1) / N_Q_HEADS)
                masked = jnp.where(kj == qi, sinks_ref[h] * LOG2_E, NEG_INF * LOG2_E)
                logit_ref[first, h] = jnp.where(ok, (-slope * LOG2_E) * dist_f, masked)
        q_end = POOL_WIDTH + ATTN_WIDTH
        w_in_ref[:, :POOL_WIDTH] = w_in_f32_ref[:, :POOL_WIDTH].astype(jnp.bfloat16)
        w_in_ref[:, q_end:] = w_in_f32_ref[:, q_end:].astype(jnp.bfloat16)
        low_w = lax.broadcasted_iota(jnp.int32, (w_in_ref.shape[0], 2 * HEAD_DIM), 1) < HEAD_DIM
        for j, (h0, h1) in enumerate(PAIR_HEADS):
            src0, src1 = (POOL_WIDTH + (h // 2) * 2 * HEAD_DIM for h in (h0, h1))
            a = w_in_f32_ref[:, src0:src0 + 2 * HEAD_DIM]
            b = w_in_f32_ref[:, src1:src1 + 2 * HEAD_DIM]
            if h0 % 2 == 0:
                pair = jnp.where(low_w, a, pltpu.roll(b, HEAD_DIM, axis=1))
            else:
                pair = jnp.where(low_w, pltpu.roll(a, HEAD_DIM, axis=1), b)
            col = POOL_WIDTH + j * 2 * HEAD_DIM
            w_in_ref[:, col:col + 2 * HEAD_DIM] = pair.astype(jnp.bfloat16)
        for g in range(len(POOL_WINDOWS)):
            cols = slice(g * POOL_GROUP, (g + 1) * POOL_GROUP)
            w_out_ref[cols, :] = (jnp.dot(
                w_pool_ref[g] * pool_scale_ref[:, cols], w_out_f32_ref[cols, :],
                precision=lax.Precision.HIGHEST, preferred_element_type=jnp.float32)
                                  * (1.0 / alpha)).astype(jnp.bfloat16)
        for i, h in enumerate(HEAD_ORDER):
            src, dst = POOL_WIDTH + h * HEAD_DIM, POOL_WIDTH + i * HEAD_DIM
            w_out_ref[dst:dst + HEAD_DIM, :] = (w_out_f32_ref[src:src + HEAD_DIM, :]
                                                * (1.0 / alpha)).astype(jnp.bfloat16)

    xb = x_ref[...].astype(jnp.bfloat16)
    half = ts // 2
    halo = halo_ref[slot]
    for r0 in (0, half):
        rows = slice(r0, r0 + half)
        z_ref[rows, :] = jnp.dot(xb[rows], w_in_ref[...], preferred_element_type=jnp.float32)

        for g, w in enumerate(POOL_WINDOWS):
            cols = slice(g * POOL_GROUP, (g + 1) * POOL_GROUP)
            u = z_ref[rows, cols]
            s = jnp.concatenate([halo[:, cols], u], axis=0)
            shift = 1
            while shift < w:
                s = s + pltpu.roll(s, shift, axis=0)
                shift *= 2
            mean = s[POOL_HALO:] * (1.0 / w)
            if r0 == 0:
                pos = tile * ts + lax.broadcasted_iota(jnp.int32, (POOL_HALO, POOL_GROUP), 0)
                count = jnp.minimum(pos + 1, w).astype(jnp.float32)
                mean = jnp.concatenate([s[POOL_HALO:2 * POOL_HALO] / count, mean[POOL_HALO:]], axis=0)
            mix_ref[rows, cols] = mean - u
        halo = z_ref[r0 + half - POOL_HALO:r0 + half, :POOL_WIDTH]
    halo_ref[1 - slot] = halo

    def attn_block(blk, prev):
        rows = slice(blk * BLOCK, (blk + 1) * BLOCK)
        table = jnp.minimum(tile * (ts // BLOCK) + blk, 1)
        valid = in_band & (kj >= BLOCK * (1 - table))

        cur = z_ref[rows, POOL_WIDTH + ATTN_WIDTH:].astype(jnp.bfloat16)
        band = jnp.concatenate([prev, cur], axis=0)
        q_parts = []
        for j, pair in enumerate(PAIR_HEADS):
            col = POOL_WIDTH + j * 2 * HEAD_DIM
            q = z_ref[rows, col:col + 2 * HEAD_DIM] * (LOG2_E / math.sqrt(HEAD_DIM))
            q_parts.append(jnp.where(low_half, q, 0.0).astype(jnp.bfloat16))
            q_parts.append(jnp.where(low_half, 0.0, q).astype(jnp.bfloat16))
        scores = lax.dot_general(jnp.concatenate(q_parts, axis=0), band[:, :KV_WIDTH],
                                 (((1,), (1,)), ((), ())),
                                 preferred_element_type=jnp.float32)
        p_parts, inv_parts = [], []
        for i, h in enumerate(HEAD_ORDER):
            sc = jnp.where(valid, scores[i * BLOCK:(i + 1) * BLOCK], 0.0) + logit_ref[table, h]
            e = jnp.exp2(sc - jnp.max(sc, axis=-1, keepdims=True))
            inv_parts.append(1.0 / jnp.sum(e, axis=-1, keepdims=True))
            p_parts.append(jnp.where(valid, e, 0.0).astype(jnp.bfloat16))
        pv = jnp.dot(jnp.concatenate(p_parts, axis=0), band[:, KV_WIDTH:],
                     preferred_element_type=jnp.float32)
        for j in range(len(PAIR_HEADS)):
            lo = pv[2 * j * BLOCK:(2 * j + 1) * BLOCK] * inv_parts[2 * j]
            hi = pv[(2 * j + 1) * BLOCK:(2 * j + 2) * BLOCK] * inv_parts[2 * j + 1]
            col = POOL_WIDTH + j * 2 * HEAD_DIM
            mix_ref[rows, col:col + 2 * HEAD_DIM] = jnp.where(low_half, lo, hi)
        return cur

    prev = kvprev_ref[slot]
    for blk in range(ts // BLOCK):
        prev = attn_block(blk, prev)
    kvprev_ref[1 - slot] = prev

    mix = jnp.dot(mix_ref[...].astype(jnp.bfloat16), w_out_ref[...], preferred_element_type=jnp.float32)
    o_ref[...] = _layer_norm_of_scaled(x_ref[...] + mix, g_ref[...], b_ref[...], alpha)


def _ffn_kernel(h_ref, p_ref, w1_ref, w2_ref, g_ref, b_ref, wple_ref, wgate_ref, bgate_ref, o_ref, *,
                alpha):
    h = h_ref[...]
    hb = h.astype(jnp.bfloat16)
    acc = h
    for c in range(D_FF // FF_CHUNK):
        cols = slice(c * FF_CHUNK, (c + 1) * FF_CHUNK)
        hid = jnp.maximum(jnp.dot(hb, w1_ref[:, cols], preferred_element_type=jnp.float32), 0.0)
        hid = hid.astype(jnp.bfloat16)
        acc = acc + jnp.dot(hid * hid, w2_ref[cols, :], preferred_element_type=jnp.float32)
    h2 = _layer_norm_of_scaled(acc, g_ref[...], b_ref[...], alpha)
    gate = jax.nn.sigmoid(
        jnp.dot(h2.astype(jnp.bfloat16), wgate_ref[...], preferred_element_type=jnp.float32)
        + bgate_ref[...])
    ple = jnp.dot(p_ref[...].astype(jnp.bfloat16), wple_ref[...], preferred_element_type=jnp.float32)
    o_ref[...] = h2 + gate * ple


def _slab_spec(shape, axis, n_steps, tiles):
    assert shape[axis] % n_steps == 0 and (shape[axis] // n_steps) % BF16_TILE[axis] == 0
    block = tuple(dim // n_steps if a == axis else dim for a, dim in enumerate(shape))
    return pl.BlockSpec(block, lambda b, i: tuple(b * tiles + i if a == axis else 0 for a in range(2)))


def _mixer_call(x, sinks, w_in, w_pool, pool_scale, w_out, ln_g, ln_b, ffn_weights, alpha):
    batch, seq, d = x.shape
    assert d == D_MODEL and seq % MIX_TILE == 0 and MIX_TILE % BLOCK == 0
    tiles = seq // MIX_TILE
    slab_specs = [_slab_spec(w.shape, axis, batch * tiles, tiles) for w, axis in zip(ffn_weights, (1, 0, 0, 0))]
    grid_spec = pl.GridSpec(
        grid=(batch, tiles),
        in_specs=[
            pl.BlockSpec(memory_space=pltpu.SMEM),
            pl.BlockSpec((None, MIX_TILE, d), lambda b, i: (b, i, 0)),
            _const_spec((d, IN_WIDTH)),
            _const_spec((len(POOL_WINDOWS), POOL_GROUP, POOL_GROUP)),
            _const_spec((1, POOL_WIDTH)),
            _const_spec((POOL_WIDTH + ATTN_WIDTH, d)),
            _const_spec((1, d)),
            _const_spec((1, d)),
            *slab_specs,
        ],
        out_specs=[pl.BlockSpec((None, MIX_TILE, d), lambda b, i: (b, i, 0)), *slab_specs],
        scratch_shapes=[
            pltpu.VMEM((MIX_TILE, IN_WIDTH), jnp.float32),
            pltpu.VMEM((MIX_TILE, POOL_WIDTH + ATTN_WIDTH), jnp.float32),
            pltpu.VMEM((2, POOL_HALO, POOL_WIDTH), jnp.float32),
            pltpu.VMEM((2, BLOCK, 2 * KV_WIDTH), jnp.bfloat16),
            pltpu.VMEM((2, N_Q_HEADS, BLOCK, 2 * BLOCK), jnp.float32),
            pltpu.VMEM((d, IN_WIDTH), jnp.bfloat16),
            pltpu.VMEM((POOL_WIDTH + ATTN_WIDTH, d), jnp.bfloat16),
        ],
    )
    return pl.pallas_call(
        functools.partial(_mixer_kernel, alpha=alpha),
        grid_spec=grid_spec,
        out_shape=[jax.ShapeDtypeStruct(x.shape, jnp.float32),
                   *(jax.ShapeDtypeStruct(w.shape, jnp.bfloat16) for w in ffn_weights)],
        compiler_params=pltpu.CompilerParams(
            dimension_semantics=("arbitrary", "arbitrary"), vmem_limit_bytes=VMEM_LIMIT_BYTES),
        name="mixer",
    )(sinks, x, w_in, w_pool, pool_scale, w_out, ln_g, ln_b, *ffn_weights)


def _ffn_call(h, p, w1, w2, ln_g, ln_b, w_ple, w_gate, b_gate, alpha):
    batch, seq, d = h.shape
    assert seq % FFN_TILE == 0
    return pl.pallas_call(
        functools.partial(_ffn_kernel, alpha=alpha),
        grid=(batch, seq // FFN_TILE),
        in_specs=[
            pl.BlockSpec((None, FFN_TILE, d), lambda b, i: (b, i, 0)),
            pl.BlockSpec((None, FFN_TILE, PLE_DIM), lambda b, i: (b, i, 0)),
            _const_spec((d, D_FF)),
            _const_spec((D_FF, d)),
            _const_spec((1, d)),
            _const_spec((1, d)),
            _const_spec((PLE_DIM, d)),
            _const_spec((d, d)),
            _const_spec((1, d)),
        ],
        out_specs=pl.BlockSpec((None, FFN_TILE, d), lambda b, i: (b, i, 0)),
        out_shape=jax.ShapeDtypeStruct(h.shape, jnp.float32),
        compiler_params=pltpu.CompilerParams(
            dimension_semantics=("arbitrary", "arbitrary"), vmem_limit_bytes=VMEM_LIMIT_BYTES),
        name="ffn",
    )(h, p, w1, w2, ln_g, ln_b, w_ple, w_gate, b_gate)


def kernel(x, p, w_in, w_pool, pool_scale, attn_sinks, w_out, ln1_g, ln1_b, w_ff1, w_ff2, ln2_g, ln2_b,
           w_ple, w_ple_gate, b_ple_gate):
    depth = w_in.shape[0]
    alpha = (2 * depth) ** 0.25
    row = lambda v: v.reshape(1, -1)
    h = x
    for i in range(depth):
        h, w1, w2, wple, wgate = _mixer_call(
            h, attn_sinks[i], w_in[i], w_pool[i], row(pool_scale[i]), w_out[i],
            row(ln1_g[i]), row(ln1_b[i]), (w_ff1[i], w_ff2[i], w_ple[i], w_ple_gate[i]), alpha)
        h = _ffn_call(h, p[i], w1, w2, row(ln2_g[i]), row(ln2_b[i]), wple, wgate, row(b_ple_gate[i]), alpha)
    return h
```
